```python
import math
import jax, jax.numpy as jnp
from jax import lax
import numpy as np

D_MODEL = 1024
BATCH = 2
SEQ = 8192
DEPTH = 1
DEC_BATCH = 128
DEC_SEQ = 1
PAST_LEN = 8192
PAGE_SIZE = 128

HEAD_DIM = 64
A_HEADS = 8
A_KV_HEADS = 4
A_GROUP = A_HEADS // A_KV_HEADS
D_ATTN = A_HEADS * HEAD_DIM
IDX_HEADS = 8
IDX_DIM = 64
TOPK_MAX = 256
Q_BLOCK = 128
N_BUCKETS = 32
MAX_DISTANCE = 128
H_HEADS = 8
H_KEY = 64
H_VAL = 64
D_HKEY = H_HEADS * H_KEY
D_HGRN = H_HEADS * H_VAL
HG_CHUNK = 64
D_MIX = D_ATTN + D_HGRN
EPS = 1e-6
SPLIT_WIDTHS = (D_ATTN, A_KV_HEADS * HEAD_DIM, A_KV_HEADS * HEAD_DIM, D_ATTN,
                IDX_HEADS * IDX_DIM, IDX_HEADS, IDX_DIM,
                D_HKEY, D_HKEY, D_HGRN, D_HGRN)
D_IN = sum(SPLIT_WIDTHS)

kernel_name = 'hymba_dsa_hgrn2_step'

F32 = jnp.float32


def rms_norm(x, g):
    xf = x.astype(F32)
    y = xf * lax.rsqrt(jnp.mean(xf * xf, axis=-1, keepdims=True) + EPS) * g.astype(F32)
    return y.astype(x.dtype)


def split_columns(h):
    parts, o = [], 0
    for w in SPLIT_WIDTHS:
        parts.append(h[..., o:o + w])
        o += w
    return parts


def t5_bucket(n):
    n = jnp.maximum(n, 0)
    max_exact = N_BUCKETS // 2
    nf = jnp.maximum(n, 1).astype(F32)
    large = max_exact + (jnp.log(nf / max_exact) / math.log(MAX_DISTANCE / max_exact)
                         * (N_BUCKETS - max_exact)).astype(jnp.int32)
    large = jnp.minimum(large, N_BUCKETS - 1)
    return jnp.where(n < max_exact, n, large)


def branch_inputs(x, ln_g, w_in, q_g, k_g, lb):
    B, T = x.shape[:2]
    h = jnp.einsum('btd,de->bte', rms_norm(x, ln_g), w_in)
    a_q, a_k, a_v, a_gate, i_q, i_w, i_k, h_q, h_f, h_i, h_gate = split_columns(h)
    q = rms_norm(a_q.reshape(B, T, A_HEADS, HEAD_DIM), q_g)
    k = rms_norm(a_k.reshape(B, T, A_KV_HEADS, HEAD_DIM), k_g)
    v = a_v.reshape(B, T, A_KV_HEADS, HEAD_DIM)
    iq = i_q.reshape(B, T, IDX_HEADS, IDX_DIM)
    iw = i_w.astype(F32) * IDX_HEADS ** -0.5
    hq = jax.nn.silu(h_q.astype(F32)).reshape(B, T, H_HEADS, H_KEY) * H_KEY ** -0.5
    lb = lb.reshape(H_HEADS, H_KEY)
    f = lb + (1.0 - lb) * jax.nn.sigmoid(h_f.astype(F32).reshape(B, T, H_HEADS, H_KEY))
    hg = jnp.log(f)
    hk = 1.0 - f
    hv = h_i.astype(F32).reshape(B, T, H_HEADS, H_VAL)
    return q, k, v, a_gate, iq, iw, i_k, hq, hg, hk, hv, h_gate


def indexer_topk(iq, iw, ik, q_pos, topk):
    s = jnp.einsum('bthd,bsd->bths', iq.astype(F32), ik.astype(F32)) * IDX_DIM ** -0.5
    score = jnp.einsum('bths,bth->bts', jax.nn.relu(s), iw)
    L = ik.shape[1]
    causal = jnp.arange(L)[None, :] <= q_pos[:, None]
    score = jnp.where(causal[None], score, -jnp.inf)
    _, idx = lax.top_k(score, topk)
    valid = idx <= q_pos[None, :, None]
    return idx, valid


def sparse_attn(q, k_sel, v_sel, idx, q_pos, valid, rel_table):
    B, T = q.shape[:2]
    K = idx.shape[-1]
    qg = q.reshape(B, T, A_KV_HEADS, A_GROUP, HEAD_DIM)
    logits = jnp.einsum('btngd,btknd->btngk', qg, k_sel).astype(F32) * HEAD_DIM ** -0.5
    bias = rel_table.astype(F32)[t5_bucket(q_pos[None, :, None] - idx)]
    bias = bias.reshape(B, T, K, A_KV_HEADS, A_GROUP).transpose(0, 1, 3, 4, 2)
    logits = jnp.where(valid[:, :, None, None, :], logits + bias, -jnp.inf)
    p = jax.nn.softmax(logits, axis=-1)
    out = jnp.einsum('btngk,btknd->btngd', p.astype(v_sel.dtype), v_sel)
    return out.reshape(B, T, D_ATTN)


def attn_prompt(q, k, v, iq, iw, ik, rel_table):
    B, S = q.shape[:2]
    topk = min(TOPK_MAX, S // 4)
    nb = S // Q_BLOCK
    take = jax.vmap(lambda rows, ids: rows[ids])

    def block(args):
        qb, iqb, iwb, start = args
        q_pos = start + jnp.arange(Q_BLOCK, dtype=jnp.int32)
        idx, valid = indexer_topk(iqb, iwb, ik, q_pos, topk)
        return sparse_attn(qb, take(k, idx), take(v, idx), idx, q_pos, valid, rel_table)

    to_blocks = lambda a: jnp.moveaxis(a.reshape((B, nb, Q_BLOCK) + a.shape[2:]), 1, 0)
    starts = jnp.arange(nb, dtype=jnp.int32) * Q_BLOCK
    out = lax.map(block, (to_blocks(q), to_blocks(iq), to_blocks(iw), starts))
    return jnp.moveaxis(out, 0, 1).reshape(B, S, D_ATTN)


def attn_sample(q, k_new, v_new, iq, iw, ik_new, cache_k, cache_v, cache_ik, layer,
                page_table, rel_table):
    DB, T = q.shape[:2]
    past = page_table.shape[1] * PAGE_SIZE
    ik_past = cache_ik[layer, page_table].reshape(DB, past, IDX_DIM)
    ik_all = jnp.concatenate([ik_past, ik_new.astype(ik_past.dtype)], axis=1)
    topk = min(TOPK_MAX, (past + T) // 4)
    q_pos = past + jnp.arange(T, dtype=jnp.int32)
    idx, valid = indexer_topk(iq, iw, ik_all, q_pos, topk)
    in_past = idx < past
    jp = jnp.minimum(idx, past - 1)
    phys = jax.vmap(lambda pt, pg: pt[pg])(page_table, jp // PAGE_SIZE)
    off = jp % PAGE_SIZE
    jn = jnp.clip(idx - past, 0, T - 1)
    take = jax.vmap(lambda rows, ids: rows[ids])

    def select(pool, new):
        from_past = pool[layer, phys, off]
        from_new = take(new, jn).astype(from_past.dtype)
        return jnp.where(in_past[..., None, None], from_past, from_new)

    return sparse_attn(q, select(cache_k, k_new), select(cache_v, v_new), idx, q_pos,
                       valid, rel_table)


def gla_chunk(S0, q, g, k, v):
    C = q.shape[1]
    b = jnp.cumsum(g, axis=1)
    mask = jnp.tril(jnp.ones((C, C), dtype=bool))[None, :, :, None, None]
    diff = b[:, :, None] - b[:, None, :]
    decay = jnp.exp(jnp.where(mask, diff, -jnp.inf))
    A = jnp.einsum('bthk,bshk,btshk->bths', q, k, decay)
    o = jnp.einsum('bths,bshv->bthv', A, v) + jnp.einsum('bthk,bhkv->bthv', q * jnp.exp(b), S0)
    S_new = jnp.exp(b[:, -1])[..., None] * S0 + jnp.einsum(
        'bshk,bshv->bhkv', k * jnp.exp(b[:, -1:] - b), v)
    return S_new, o


def hgrn_prompt(q, g, k, v):
    B, S = q.shape[:2]
    nc = S // HG_CHUNK
    to_c = lambda a: jnp.moveaxis(a.reshape((B, nc, HG_CHUNK) + a.shape[2:]), 1, 0)
    S0 = jnp.zeros((B, H_HEADS, H_KEY, H_VAL), F32)
    S_fin, o = lax.scan(lambda s, xs: gla_chunk(s, *xs), S0, (to_c(q), to_c(g), to_c(k), to_c(v)))
    return S_fin, jnp.moveaxis(o, 0, 1).reshape(B, S, H_HEADS, H_VAL)


def out_merge(x, a_out, a_gate, h_out, h_gate, hn_g, w_out):
    B, T = x.shape[:2]
    h_n = rms_norm(h_out, hn_g).reshape(B, T, D_HGRN).astype(x.dtype)
    mix = jnp.concatenate([a_out.astype(x.dtype) * jax.nn.silu(a_gate),
                           h_n * jax.nn.silu(h_gate)], axis=-1)
    return x + jnp.einsum('bte,ed->btd', mix, w_out)


def setup_inputs(seed: int = 0) -> dict:
    key = jax.random.key(seed)
    ks = jax.random.split(key, 16)
    n_pages = PAST_LEN // PAGE_SIZE
    n_pool = (DEC_BATCH * n_pages * 5) // 4
    nrm = jax.random.normal
    page_table = jax.random.permutation(ks[0], n_pool)[:DEC_BATCH * n_pages]
    page_table = page_table.reshape(DEC_BATCH, n_pages).astype(jnp.int32)
    return {
        'x_prompt': nrm(ks[1], (BATCH, SEQ, D_MODEL), F32),
        'x_sample': nrm(ks[2], (DEC_BATCH, DEC_SEQ, D_MODEL), F32),
        'cache_k': nrm(ks[3], (DEPTH, n_pool, PAGE_SIZE, A_KV_HEADS, HEAD_DIM), F32),
        'cache_v': nrm(ks[4], (DEPTH, n_pool, PAGE_SIZE, A_KV_HEADS, HEAD_DIM), F32),
        'cache_ik': nrm(ks[5], (DEPTH, n_pool, PAGE_SIZE, IDX_DIM), F32),
        'state_hgrn': 0.5 * nrm(ks[6], (DEPTH, DEC_BATCH, H_HEADS, H_KEY, H_VAL), F32),
        'page_table': page_table,
        'rel_bias': 0.5 * nrm(ks[7], (N_BUCKETS, A_HEADS), F32),
        'ln_g': 1.0 + 0.1 * nrm(ks[8], (DEPTH, D_MODEL), F32),
        'w_in': nrm(ks[9], (DEPTH, D_MODEL, D_IN), F32) * D_MODEL ** -0.5,
        'q_norm_g': 1.0 + 0.1 * nrm(ks[10], (DEPTH, HEAD_DIM), F32),
        'k_norm_g': 1.0 + 0.1 * nrm(ks[11], (DEPTH, HEAD_DIM), F32),
        'hgrn_lb': nrm(ks[12], (DEPTH + 1, D_HKEY), F32),
        'hgrn_norm_g': 1.0 + 0.1 * nrm(ks[13], (DEPTH, H_VAL), F32),
        'w_out': nrm(ks[14], (DEPTH, D_MIX, D_MODEL), F32) * D_MIX ** -0.5,
    }


def reference(x_prompt, x_sample, cache_k, cache_v, cache_ik, state_hgrn, page_table,
              rel_bias, ln_g, w_in, q_norm_g, k_norm_g, hgrn_lb, hgrn_norm_g, w_out):
    lb_all = jnp.cumsum(jax.nn.softmax(hgrn_lb.astype(F32), axis=0), axis=0)
    xp, xs = x_prompt, x_sample
    kp_l, vp_l, ikp_l, sp_l, ks_l, vs_l, iks_l, ss_l = [], [], [], [], [], [], [], []
    for l in range(DEPTH):
        q, k, v, a_gate, iq, iw, ik, hq, hg, hk, hv, h_gate = branch_inputs(
            xp, ln_g[l], w_in[l], q_norm_g[l], k_norm_g[l], lb_all[l])
        a_out = attn_prompt(q, k, v, iq, iw, ik, rel_bias)
        s_fin, h_out = hgrn_prompt(hq, hg, hk, hv)
        xp = out_merge(xp, a_out, a_gate, h_out, h_gate, hgrn_norm_g[l], w_out[l])
        kp_l.append(k); vp_l.append(v); ikp_l.append(ik); sp_l.append(s_fin.astype(x_prompt.dtype))
        q, k, v, a_gate, iq, iw, ik, hq, hg, hk, hv, h_gate = branch_inputs(
            xs, ln_g[l], w_in[l], q_norm_g[l], k_norm_g[l], lb_all[l])
        a_out = attn_sample(q, k, v, iq, iw, ik, cache_k, cache_v, cache_ik, l, page_table, rel_bias)
        s_new, h_out = gla_chunk(state_hgrn[l].astype(F32), hq, hg, hk, hv)
        xs = out_merge(xs, a_out, a_gate, h_out, h_gate, hgrn_norm_g[l], w_out[l])
        ks_l.append(k); vs_l.append(v); iks_l.append(ik); ss_l.append(s_new.astype(state_hgrn.dtype))
    return (xp, xs, jnp.stack(kp_l), jnp.stack(vp_l), jnp.stack(ikp_l), jnp.stack(sp_l),
            jnp.stack(ks_l), jnp.stack(vs_l), jnp.stack(iks_l), jnp.stack(ss_l))
```

```python
import functools
import math

import jax
import jax.numpy as jnp
from jax import lax
from jax.experimental import pallas as pl
from jax.experimental.pallas import tpu as pltpu

F32 = jnp.float32
BF16 = jnp.bfloat16
MXU_DTYPE = BF16

HEAD_DIM = 64
A_HEADS = 8
A_KV_HEADS = 4
D_ATTN = A_HEADS * HEAD_DIM
D_KV = A_KV_HEADS * HEAD_DIM
IDX_HEADS = 8
IDX_DIM = 64
H_HEADS = 8
H_KEY = 64
D_H = H_HEADS * H_KEY
TOPK = 256
PAGE = 128
N_BUCKETS = 32
MAX_DISTANCE = 128
EPS = 1e-6
LANES = 128
Q_BLK = 128
K_CHUNK = 512
HG_BLK = 256
HG_CHUNK = 16
HG_SUB = 32
PAGES_PER_STEP = 8
VMEM_LIMIT = 48 * 1024 * 1024
NEG_INF = float("-inf")
KEY_NEG_INF = -2139095041

C_AQ, C_AK, C_AV, C_AG, C_IQ, C_IK, C_IW, C_HQ, C_HF, C_HI, C_HG, C_END = (
    0, 512, 768, 1024, 1536, 2048, 2176, 2304, 2816, 3328, 3840, 4352)
HEAD_PERM = (0, 2, 1, 3, 4, 6, 5, 7)


def _mxu(x):
    return x.astype(MXU_DTYPE)


def _dot(a, b):
    return jnp.dot(a, b, preferred_element_type=F32)


def _dot_nt(a, b):
    return lax.dot_general(a, b, (((1,), (1,)), ((), ())), preferred_element_type=F32)


def _dot_tn(a, b):
    return lax.dot_general(a, b, (((0,), (0,)), ((), ())), preferred_element_type=F32)


def _split_dot(x, w, parts):
    acc = None
    r = x
    for _ in range(parts):
        p = _mxu(r)
        t = _dot(p, w)
        acc = t if acc is None else acc + t
        r = r - p.astype(F32)
    return acc


def _sigmoid(x):
    return 1.0 / (1.0 + jnp.exp(-x))


def _silu(x):
    return x * _sigmoid(x)


def _t5_bucket(n):
    n = jnp.maximum(n, 0)
    max_exact = N_BUCKETS // 2
    nf = jnp.maximum(n, 1).astype(F32)
    large = max_exact + (jnp.log(nf / max_exact) / math.log(MAX_DISTANCE / max_exact)
                         * (N_BUCKETS - max_exact)).astype(jnp.int32)
    large = jnp.minimum(large, N_BUCKETS - 1)
    return jnp.where(n < max_exact, n, large)


def _block_diag(n, blk, val, dtype):
    r = lax.broadcasted_iota(jnp.int32, (n, n), 0) // blk
    c = lax.broadcasted_iota(jnp.int32, (n, n), 1) // blk
    return jnp.where(r == c, val, 0.0).astype(dtype)


def _proj_kernel(x_ref, lng_ref, w_ref, qg_ref, kg_ref, lb_ref, bd_ref,
                 q_ref, iq_ref, kf_ref, kb_ref, vf_ref, vb_ref, ikw_ref, ik2_ref,
                 sga_ref, hq_ref, hg_ref, hk_ref, hv_ref, sgh_ref):
    x = x_ref[...]
    ms = jnp.mean(x * x, axis=-1, keepdims=True)
    xb = _mxu(x * lax.rsqrt(ms + EPS) * lng_ref[...])

    def seg(a, b):
        return _dot(xb, w_ref[:, a:b])

    bd = bd_ref[...]

    aq = seg(C_AQ, C_AK)
    msq = _split_dot(aq * aq, bd, 2)
    q_ref[...] = (aq * lax.rsqrt(msq + EPS) * qg_ref[...] * HEAD_DIM ** -0.5).astype(q_ref.dtype)

    ak = seg(C_AK, C_AV)
    msk = _split_dot(ak * ak, bd[:D_KV, :D_KV], 2)
    k = ak * lax.rsqrt(msk + EPS) * kg_ref[...]
    kf_ref[...] = k
    kb_ref[...] = k.astype(kb_ref.dtype)

    v = seg(C_AV, C_AG)
    vf_ref[...] = v
    vb_ref[...] = v.astype(vb_ref.dtype)

    sga_ref[...] = _silu(seg(C_AG, C_IQ))
    iq_ref[...] = seg(C_IQ, C_IK).astype(iq_ref.dtype)

    ikk = seg(C_IK, C_IW)
    iww = seg(C_IW, C_HQ) * (IDX_HEADS ** -0.5 * IDX_DIM ** -0.5)
    ik2_ref[...] = ikk.astype(ik2_ref.dtype)
    lane = lax.broadcasted_iota(jnp.int32, ikk.shape, 1)
    ikw_ref[...] = jnp.where(lane < IDX_DIM, ikk, iww)

    hq_ref[...] = _silu(seg(C_HQ, C_HF)) * H_KEY ** -0.5
    lbp = lb_ref[...]
    mx = jnp.max(lbp, axis=0, keepdims=True)
    e = jnp.exp(lbp - mx)
    lb = e[0:1] / jnp.sum(e, axis=0, keepdims=True)
    f = lb + (1.0 - lb) * _sigmoid(seg(C_HF, C_HI))
    hg_ref[...] = jnp.log(f)
    hk_ref[...] = 1.0 - f
    hv_ref[...] = seg(C_HI, C_HG)
    sgh_ref[...] = _silu(seg(C_HG, C_END))


def _project(x2d, ln_g, w_pack, qg, kg, lbp, bd, tm):
    n, d = x2d.shape
    row = lambda w: pl.BlockSpec((tm, w), lambda i: (i, 0))
    full = lambda a: pl.BlockSpec(a.shape, lambda i: (0,) * a.ndim)
    outs = [("q", D_ATTN, MXU_DTYPE), ("iq", D_ATTN, MXU_DTYPE), ("kf", D_KV, F32),
            ("kb", D_KV, MXU_DTYPE), ("vf", D_KV, F32), ("vb", D_KV, MXU_DTYPE),
            ("ikw", LANES, F32), ("ik2", LANES, MXU_DTYPE), ("sga", D_ATTN, F32),
            ("hq", D_H, F32), ("hg", D_H, F32), ("hk", D_H, F32), ("hv", D_H, F32),
            ("sgh", D_H, F32)]
    res = pl.pallas_call(
        _proj_kernel,
        grid=(n // tm,),
        in_specs=[row(d), full(ln_g), full(w_pack), full(qg), full(kg), full(lbp), full(bd)],
        out_specs=[row(w) for _, w, _ in outs],
        out_shape=[jax.ShapeDtypeStruct((n, w), dt) for _, w, dt in outs],
        compiler_params=pltpu.CompilerParams(
            dimension_semantics=("arbitrary",), vmem_limit_bytes=VMEM_LIMIT),
        name="proj",
    )(x2d, ln_g, w_pack, qg, kg, lbp, bd)
    return dict(zip([o[0] for o in outs], res))


def _bias_tile_kernel(tab_ref, o_ref):
    r = lax.broadcasted_iota(jnp.int32, (Q_BLK, 2 * Q_BLK), 0)
    j = lax.broadcasted_iota(jnp.int32, (Q_BLK, 2 * Q_BLK), 1)
    bucket = _t5_bucket(Q_BLK + r - j)
    for h in range(A_HEADS):
        acc = jnp.zeros((Q_BLK, 2 * Q_BLK), F32)
        for b in range(N_BUCKETS):
            acc = jnp.where(bucket == b, tab_ref[b, h], acc)
        o_ref[h] = acc


def _bias_tiles(rel_bias):
    return pl.pallas_call(
        _bias_tile_kernel,
        in_specs=[pl.BlockSpec(memory_space=pltpu.SMEM)],
        out_shape=jax.ShapeDtypeStruct((A_HEADS, Q_BLK, 2 * Q_BLK), F32),
        name="bias_tiles",
    )(rel_bias)


def _key_to_f32(key):
    bits = key ^ ((key >> 31) & jnp.int32(0x7FFFFFFF))
    return lax.bitcast_convert_type(bits, F32)


def _topk_cut(sc_ref, nch, rows, cut_ref, extra=None):
    ncols = nch * K_CHUNK
    nbits = int(sc_ref.shape[1]).bit_length()

    def count(pred):
        def body(c, acc):
            r0 = pl.multiple_of(c * K_CHUNK, K_CHUNK)
            blk = sc_ref[:, pl.ds(r0, K_CHUNK)]
            for j in range(K_CHUNK // LANES):
                acc = acc + jnp.where(pred(blk[:, j * LANES:(j + 1) * LANES], r0 + j * LANES), 1, 0)
            return acc

        acc = lax.fori_loop(0, nch, body, jnp.zeros((rows, LANES), jnp.int32))
        cnt = jnp.sum(acc, axis=1, keepdims=True)
        if extra is not None:
            cnt = cnt + jnp.where(pred(extra, ncols), 1, 0)
        return cnt

    def bit_body(it, carry):
        key, n_ge = carry
        cand = key + jnp.left_shift(jnp.int32(1), 31 - it)
        cb = jnp.broadcast_to(_key_to_f32(cand), (rows, LANES))
        cnt = count(lambda v, c0: v >= cb[:, :v.shape[1]])
        ok = cnt >= TOPK
        return jnp.where(ok, cand, key), jnp.where(ok, cnt, n_ge)

    key0 = jnp.full((rows, 1), jnp.iinfo(jnp.int32).min, jnp.int32)
    key, n_ge = lax.fori_loop(0, 32, bit_body, (key0, jnp.full((rows, 1), TOPK, jnp.int32)))
    below = key < KEY_NEG_INF
    thr = _key_to_f32(jnp.maximum(key, KEY_NEG_INF))
    cut_ref[...] = jnp.full((rows, 1), jnp.iinfo(jnp.int32).max, jnp.int32)

    @pl.when(jnp.max(jnp.where(below, TOPK + 1, n_ge)) > TOPK)
    def _():
        tb = jnp.broadcast_to(thr, (rows, LANES))
        need = TOPK - count(lambda v, c0: v > tb[:, :v.shape[1]])

        def col_of(v, c0):
            return c0 + lax.broadcasted_iota(jnp.int32, v.shape, 1)

        def idx_body(it, x):
            cand = x + jnp.left_shift(jnp.int32(1), nbits - 1 - it)
            cb = jnp.broadcast_to(cand, (rows, LANES))
            g = count(lambda v, c0: (v == tb[:, :v.shape[1]]) & (col_of(v, c0) < cb[:, :v.shape[1]]))
            return jnp.where(g < need, cand, x)

        cut_ref[...] = lax.fori_loop(0, nbits, idx_body, jnp.zeros((rows, 1), jnp.int32))

    return thr


def _in_topk(sc, col, thr, cut):
    return (sc > thr) | ((sc == thr) & (col <= cut))


def _lane_halves(x):
    lo = lax.broadcasted_iota(jnp.int32, x.shape, 1) < HEAD_DIM
    zero = jnp.zeros_like(x)
    return jnp.where(lo, x, zero), jnp.where(lo, zero, x)


def _softmax_step(x, hidx, m_ref, l_ref):
    m_old = m_ref[hidx]
    m_new = jnp.maximum(m_old, jnp.max(x, axis=1, keepdims=True))
    m_safe = jnp.where(m_new == NEG_INF, 0.0, m_new)
    alpha = jnp.exp(m_old - m_safe)
    p = jnp.exp(x - m_safe)
    l_ref[hidx] = alpha * l_ref[hidx] + jnp.sum(p, axis=1, keepdims=True)
    m_ref[hidx] = m_new
    return p, alpha


def _attn_prompt_kernel(tab_ref, q_ref, iq_ref, iw_ref, kb_ref, vb_ref, ik2_ref, bt_ref,
                        o_ref, sc_ref, cut_ref, m_ref, l_ref, acc_ref):
    i = pl.program_id(1)
    t_pos = i * Q_BLK + lax.broadcasted_iota(jnp.int32, (Q_BLK, 1), 0)

    iqv = iq_ref[0]
    parts = []
    for g in range(D_ATTN // LANES):
        parts.extend(_lane_halves(iqv[:, g * LANES:(g + 1) * LANES]))
    iq_stack = jnp.concatenate(parts, axis=0)
    iw = iw_ref[0]
    nch = (i + 5) // 4

    def score_body(c, carry):
        r0 = pl.multiple_of(c * K_CHUNK, K_CHUNK)
        s_all = _dot_nt(iq_stack, ik2_ref[0, pl.ds(r0, K_CHUNK), :])
        sc = jnp.zeros((Q_BLK, K_CHUNK), F32)
        for h in range(IDX_HEADS):
            sc = sc + jnp.maximum(s_all[h * Q_BLK:(h + 1) * Q_BLK], 0.0) * iw[:, h:h + 1]
        s_glob = r0 - Q_BLK + lax.broadcasted_iota(jnp.int32, (Q_BLK, K_CHUNK), 1)
        valid = (s_glob >= 0) & (s_glob <= t_pos)
        sc_ref[:, pl.ds(r0, K_CHUNK)] = jnp.where(valid, sc, NEG_INF)
        return carry

    lax.fori_loop(0, nch, score_body, 0)
    thr = _topk_cut(sc_ref, nch, Q_BLK, cut_ref)
    cut = cut_ref[...]

    m_ref[...] = jnp.full(m_ref.shape, NEG_INF, F32)
    l_ref[...] = jnp.zeros(l_ref.shape, F32)
    acc_ref[...] = jnp.zeros(acc_ref.shape, F32)

    qv = q_ref[0]
    q_pairs = []
    for p in range(2):
        g0 = _lane_halves(qv[:, (2 * p) * LANES:(2 * p + 1) * LANES])
        g1 = _lane_halves(qv[:, (2 * p + 1) * LANES:(2 * p + 2) * LANES])
        q_pairs.append(jnp.concatenate([g0[0], g1[0], g0[1], g1[1]], axis=0))

    def attend(r0, width, valid, bias_fn):
        maskadd = jnp.where(valid, 0.0, NEG_INF)
        for p in range(2):
            kc = kb_ref[0, pl.ds(r0, width), p * LANES:(p + 1) * LANES]
            vc = vb_ref[0, pl.ds(r0, width), p * LANES:(p + 1) * LANES]
            lg = _dot_nt(q_pairs[p], kc)
            ps, alphas = [], []
            for a in range(4):
                hidx = 4 * p + a
                x = lg[a * Q_BLK:(a + 1) * Q_BLK] + bias_fn(hidx) + maskadd
                pe, alpha = _softmax_step(x, hidx, m_ref, l_ref)
                ps.append(_mxu(pe))
                alphas.append(alpha)
            pv = _dot(jnp.concatenate(ps, axis=0), vc)
            for a in range(4):
                hidx = 4 * p + a
                acc_ref[hidx] = alphas[a] * acc_ref[hidx] + pv[a * Q_BLK:(a + 1) * Q_BLK]

    def far_body(c, carry):
        r0 = pl.multiple_of(c * K_CHUNK, K_CHUNK)
        row = r0 + lax.broadcasted_iota(jnp.int32, (Q_BLK, K_CHUNK), 1)
        valid = ((row >= Q_BLK) & (row < i * Q_BLK)
                 & _in_topk(sc_ref[:, pl.ds(r0, K_CHUNK)], row, thr, cut))
        attend(r0, K_CHUNK, valid, lambda h: tab_ref[N_BUCKETS - 1, h])
        return carry

    lax.fori_loop(0, (i + 3) // 4, far_body, 0)

    r0 = pl.multiple_of(i * Q_BLK, Q_BLK)
    s_glob = r0 - Q_BLK + lax.broadcasted_iota(jnp.int32, (Q_BLK, 2 * Q_BLK), 1)
    valid = ((s_glob >= 0) & (s_glob <= t_pos)
             & _in_topk(sc_ref[:, pl.ds(r0, 2 * Q_BLK)], s_glob + Q_BLK, thr, cut))
    attend(r0, 2 * Q_BLK, valid, lambda h: bt_ref[h])

    lo = lax.broadcasted_iota(jnp.int32, (Q_BLK, LANES), 1) < HEAD_DIM
    for p in range(2):
        outs = [acc_ref[4 * p + a] / l_ref[4 * p + a] for a in range(4)]
        o_ref[0, :, (2 * p) * LANES:(2 * p + 1) * LANES] = jnp.where(lo, outs[0], outs[2])
        o_ref[0, :, (2 * p + 1) * LANES:(2 * p + 2) * LANES] = jnp.where(lo, outs[1], outs[3])


def _attn_prompt(rel_bias, bias_tiles, q, iq, iw, kbp, vbp, ik2p):
    b, s, _ = q.shape
    lp = kbp.shape[1]
    blk = lambda w: pl.BlockSpec((1, Q_BLK, w), lambda bi, i: (bi, i, 0))
    res = lambda w: pl.BlockSpec((1, lp, w), lambda bi, i: (bi, 0, 0))
    return pl.pallas_call(
        _attn_prompt_kernel,
        grid=(b, s // Q_BLK),
        in_specs=[pl.BlockSpec(memory_space=pltpu.SMEM),
                  blk(D_ATTN), blk(D_ATTN), blk(IDX_HEADS),
                  res(D_KV), res(D_KV), res(LANES),
                  pl.BlockSpec(bias_tiles.shape, lambda bi, i: (0, 0, 0))],
        out_specs=blk(D_ATTN),
        out_shape=jax.ShapeDtypeStruct((b, s, D_ATTN), F32),
        scratch_shapes=[pltpu.VMEM((Q_BLK, lp), F32),
                        pltpu.VMEM((Q_BLK, 1), jnp.int32),
                        pltpu.VMEM((A_HEADS, Q_BLK, 1), F32),
                        pltpu.VMEM((A_HEADS, Q_BLK, 1), F32),
                        pltpu.VMEM((A_HEADS, Q_BLK, LANES), F32)],
        compiler_params=pltpu.CompilerParams(
            dimension_semantics=("arbitrary", "arbitrary"), vmem_limit_bytes=VMEM_LIMIT),
        name="attn_prompt",
    )(rel_bias, q, iq, iw, kbp, vbp, ik2p, bias_tiles)


def _hgrn_prompt_kernel(q_ref, g_ref, k_ref, v_ref, o_ref, st_ref,
                        kp_ref, bp_ref, vp_ref, oi_ref):
    j = pl.program_id(1)
    n_pairs = D_H // LANES

    @pl.when(j == 0)
    def _():
        st_ref[...] = jnp.zeros(st_ref.shape, F32)

    q = q_ref[0]
    k = k_ref[0]
    v = v_ref[0]
    r = lax.broadcasted_iota(jnp.int32, (HG_BLK, HG_BLK), 0)
    c = lax.broadcasted_iota(jnp.int32, (HG_BLK, HG_BLK), 1)
    tri = jnp.where(c <= r, 1.0, 0.0).astype(MXU_DTYPE)
    bcum = _split_dot_left(tri, g_ref[0], 3)

    pad = HG_CHUNK
    zeros_pad = jnp.zeros((pad, D_H), F32)
    kp_ref[0:pad, :] = zeros_pad
    bp_ref[0:pad, :] = zeros_pad
    vp_ref[0:pad, :] = zeros_pad
    kp_ref[pad:pad + HG_BLK, :] = k
    bp_ref[pad:pad + HG_BLK, :] = bcum
    vp_ref[pad:pad + HG_BLK, :] = v

    bd1 = _block_diag(D_H, H_KEY, 1.0, MXU_DTYPE)
    for sb in range(HG_BLK // HG_SUB):
        r0 = sb * HG_SUB
        qs = q[r0:r0 + HG_SUB]
        bs = bcum[r0:r0 + HG_SUB]
        tin = lax.broadcasted_iota(jnp.int32, (HG_SUB, D_H), 0) % HG_CHUNK
        acc = jnp.zeros((HG_SUB, D_H), F32)
        for d in range(HG_CHUNK):
            a = pad + r0 - d
            kd = kp_ref[a:a + HG_SUB, :]
            vd = vp_ref[a:a + HG_SUB, :]
            if d == 0:
                pr = qs * kd
            else:
                bd_ = bp_ref[a:a + HG_SUB, :]
                pr = jnp.where(tin >= d, qs, 0.0) * kd * jnp.exp(bs - bd_)
            acc = acc + _dot(_mxu(pr), bd1) * vd
        oi_ref[r0:r0 + HG_SUB, :] = acc

    lane_r = lax.broadcasted_iota(jnp.int32, (LANES, LANES), 0) // H_KEY
    lane_c = lax.broadcasted_iota(jnp.int32, (LANES, LANES), 1) // H_KEY
    same_head = lane_r == lane_c
    for cch in range(HG_BLK // HG_CHUNK):
        r0 = cch * HG_CHUNK
        bch = bcum[r0:r0 + HG_CHUNK]
        b0 = bp_ref[pad + r0 - 1:pad + r0, :]
        bl = bp_ref[pad + r0 + HG_CHUNK - 1:pad + r0 + HG_CHUNK, :]
        qi = _mxu(q[r0:r0 + HG_CHUNK] * jnp.exp(bch - b0))
        kl = _mxu(k[r0:r0 + HG_CHUNK] * jnp.exp(bl - bch))
        vch = _mxu(v[r0:r0 + HG_CHUNK])
        dec = jnp.exp(bl - b0)
        outs = []
        for p in range(n_pairs):
            sl = slice(p * LANES, (p + 1) * LANES)
            st = st_ref[0, p]
            outs.append(_dot_nt(qi[:, sl], _mxu(st)))
            ds = _dot_tn(vch[:, sl], kl[:, sl])
            st_ref[0, p] = st * dec[:, sl] + jnp.where(same_head, ds, 0.0)
        o_ref[0, r0:r0 + HG_CHUNK, :] = oi_ref[r0:r0 + HG_CHUNK, :] + jnp.concatenate(outs, axis=1)


def _split_dot_left(w, x, parts):
    acc = None
    r = x
    for _ in range(parts):
        p = _mxu(r)
        t = _dot(w, p)
        acc = t if acc is None else acc + t
        r = r - p.astype(F32)
    return acc


def _hgrn_prompt(hq, hg, hk, hv):
    b, s, _ = hq.shape
    blk = pl.BlockSpec((1, HG_BLK, D_H), lambda bi, j: (bi, j, 0))
    n_pairs = D_H // LANES
    return pl.pallas_call(
        _hgrn_prompt_kernel,
        grid=(b, s // HG_BLK),
        in_specs=[blk, blk, blk, blk],
        out_specs=[blk, pl.BlockSpec((1, n_pairs, LANES, LANES), lambda bi, j: (bi, 0, 0, 0))],
        out_shape=[jax.ShapeDtypeStruct((b, s, D_H), F32),
                   jax.ShapeDtypeStruct((b, n_pairs, LANES, LANES), F32)],
        scratch_shapes=[pltpu.VMEM((HG_CHUNK + HG_BLK, D_H), F32)] * 3
        + [pltpu.VMEM((HG_BLK, D_H), F32)],
        compiler_params=pltpu.CompilerParams(
            dimension_semantics=("arbitrary", "arbitrary"), vmem_limit_bytes=VMEM_LIMIT),
        name="hgrn_prompt",
    )(hq, hg, hk, hv)


def _hgrn_sample_kernel(qc_ref, gc_ref, kc_ref, v_ref, s_ref, o_ref, sn_ref):
    nb = s_ref.shape[0]
    for bi in range(nb):
        for h in range(H_HEADS):
            s0 = s_ref[bi, h]
            sn = jnp.exp(gc_ref[bi, h]) * s0 + kc_ref[bi, h] * v_ref[bi, :, h * H_KEY:(h + 1) * H_KEY]
            sn_ref[bi, h] = sn
            o_ref[bi, :, h * H_KEY:(h + 1) * H_KEY] = jnp.sum(qc_ref[bi, h] * sn, axis=0, keepdims=True)


def _hgrn_sample(hq, hg, hk, hv, state):
    db = hq.shape[0]
    nb = 8
    col = lambda a: a.reshape(db, H_HEADS, H_KEY, 1)
    cspec = pl.BlockSpec((nb, H_HEADS, H_KEY, 1), lambda i: (i, 0, 0, 0))
    rspec = pl.BlockSpec((nb, 1, D_H), lambda i: (i, 0, 0))
    sspec = pl.BlockSpec((nb, H_HEADS, H_KEY, H_KEY), lambda i: (i, 0, 0, 0))
    return pl.pallas_call(
        _hgrn_sample_kernel,
        grid=(db // nb,),
        in_specs=[cspec, cspec, cspec, rspec, sspec],
        out_specs=[rspec, sspec],
        out_shape=[jax.ShapeDtypeStruct((db, 1, D_H), F32),
                   jax.ShapeDtypeStruct(state.shape, F32)],
        compiler_params=pltpu.CompilerParams(
            dimension_semantics=("arbitrary",), vmem_limit_bytes=VMEM_LIMIT),
        name="hgrn_sample",
    )(col(hq), col(hg), col(hk), hv.reshape(db, 1, D_H), state)


def _merge_kernel(x_ref, a_ref, sga_ref, h_ref, sgh_ref, hng_ref, bd_ref, w_ref, y_ref):
    h = h_ref[...]
    msq = _split_dot(h * h, bd_ref[...], 2)
    hn = h * lax.rsqrt(msq + EPS) * hng_ref[...]
    ma = _mxu(a_ref[...] * sga_ref[...])
    mh = _mxu(hn * sgh_ref[...])
    y_ref[...] = x_ref[...] + _dot(ma, w_ref[0:D_ATTN, :]) + _dot(mh, w_ref[D_ATTN:, :])


def _merge(x2d, a_out, sga, h_out, sgh, hng, bd, w_pack, tm):
    n, d = x2d.shape
    row = lambda w: pl.BlockSpec((tm, w), lambda i: (i, 0))
    full = lambda a: pl.BlockSpec(a.shape, lambda i: (0,) * a.ndim)
    return pl.pallas_call(
        _merge_kernel,
        grid=(n // tm,),
        in_specs=[row(d), row(D_ATTN), row(D_ATTN), row(D_H), row(D_H),
                  full(hng), full(bd), full(w_pack)],
        out_specs=row(d),
        out_shape=jax.ShapeDtypeStruct((n, d), F32),
        compiler_params=pltpu.CompilerParams(
            dimension_semantics=("arbitrary",), vmem_limit_bytes=VMEM_LIMIT),
        name="merge",
    )(x2d, a_out, sga, h_out, sgh, hng, bd, w_pack)


def _sample_score_kernel(pt_ref, iqh_ref, iwc_ref, ikn_ref, *refs):
    ik_refs = refs[:PAGES_PER_STEP]
    sc_ref, sn_ref = refs[PAGES_PER_STEP:]
    iqh = iqh_ref[0]
    iwc = iwc_ref[0]
    for r in range(PAGES_PER_STEP):
        s = _dot_nt(iqh, _mxu(ik_refs[r][0, 0]))
        sc_ref[0, :, r * PAGE:(r + 1) * PAGE] = jnp.sum(
            jnp.maximum(s, 0.0) * iwc, axis=0, keepdims=True)
    prod = iqh.astype(F32) * _mxu(ikn_ref[0]).astype(F32)
    s_new = jnp.sum(prod, axis=1, keepdims=True)
    s_new = jnp.sum(jnp.maximum(s_new, 0.0) * iwc, axis=0, keepdims=True)
    sn_ref[0] = jnp.broadcast_to(s_new, (1, LANES))


def _sample_scores(page_table, iqh, iwc, ikn, cache_ik):
    db, n_pages = page_table.shape
    npg = n_pages // PAGES_PER_STEP
    page_spec = lambda r: pl.BlockSpec(
        (1, 1, PAGE, IDX_DIM), lambda b, j, pt: (0, pt[b, j * PAGES_PER_STEP + r], 0, 0))
    gs = pltpu.PrefetchScalarGridSpec(
        num_scalar_prefetch=1,
        grid=(db, npg),
        in_specs=[pl.BlockSpec((1, IDX_HEADS, IDX_DIM), lambda b, j, pt: (b, 0, 0)),
                  pl.BlockSpec((1, IDX_HEADS, 1), lambda b, j, pt: (b, 0, 0)),
                  pl.BlockSpec((1, 1, IDX_DIM), lambda b, j, pt: (b, 0, 0))]
        + [page_spec(r) for r in range(PAGES_PER_STEP)],
        out_specs=[pl.BlockSpec((1, 1, PAGES_PER_STEP * PAGE), lambda b, j, pt: (b, 0, j)),
                   pl.BlockSpec((1, 1, LANES), lambda b, j, pt: (b, 0, 0))],
    )
    return pl.pallas_call(
        _sample_score_kernel,
        grid_spec=gs,
        out_shape=[jax.ShapeDtypeStruct((db, 1, n_pages * PAGE), F32),
                   jax.ShapeDtypeStruct((db, 1, LANES), F32)],
        compiler_params=pltpu.CompilerParams(
            dimension_semantics=("arbitrary", "arbitrary"), vmem_limit_bytes=VMEM_LIMIT),
        name="sample_scores",
    )(page_table, iqh, iwc, ikn, *([cache_ik] * PAGES_PER_STEP))


def _sample_thr_kernel(sc_ref, sn_ref, thr_ref, cut_ref, cut_scr):
    rows, past = sc_ref.shape
    thr = _topk_cut(sc_ref, past // K_CHUNK, rows, cut_scr, extra=sn_ref[:, 0:1])
    thr_ref[...] = jnp.broadcast_to(thr, thr_ref.shape)
    cut_ref[...] = jnp.broadcast_to(cut_scr[...], cut_ref.shape)


def _sample_threshold(scores, s_new):
    db = scores.shape[0]
    return pl.pallas_call(
        _sample_thr_kernel,
        out_shape=[jax.ShapeDtypeStruct((db, LANES), F32),
                   jax.ShapeDtypeStruct((db, LANES), jnp.int32)],
        scratch_shapes=[pltpu.VMEM((db, 1), jnp.int32)],
        compiler_params=pltpu.CompilerParams(vmem_limit_bytes=VMEM_LIMIT),
        name="sample_threshold",
    )(scores, s_new)


def _sample_attn_kernel(pt_ref, tb_ref, qs_ref, sc_ref, thr_ref, cut_ref, sn_ref, kn_ref, vn_ref,
                        *refs):
    k_refs = refs[:PAGES_PER_STEP]
    v_refs = refs[PAGES_PER_STEP:2 * PAGES_PER_STEP]
    o_ref, m_ref, l_ref, acc_ref = refs[2 * PAGES_PER_STEP:]
    j = pl.program_id(1)
    nj = pl.num_programs(1)
    past = nj * PAGES_PER_STEP * PAGE

    @pl.when(j == 0)
    def _():
        m_ref[...] = jnp.full(m_ref.shape, NEG_INF, F32)
        l_ref[...] = jnp.zeros(l_ref.shape, F32)
        acc_ref[...] = jnp.zeros(acc_ref.shape, F32)

    thr = thr_ref[0, :, 0:1]
    cut = cut_ref[0, :, 0:1]
    qs = qs_ref[0]
    lo = lax.broadcasted_iota(jnp.int32, (1, LANES), 1) < HEAD_DIM

    def step(kp, vp, valid, bias8):
        maskadd = jnp.where(valid, 0.0, NEG_INF)
        for p in range(2):
            lg = _dot_nt(qs[p], _mxu(kp[:, p * LANES:(p + 1) * LANES]))
            xb = jnp.concatenate([bias8[4 * p:4 * p + 4], jnp.zeros((4, bias8.shape[1]), F32)], axis=0)
            x = lg + xb + maskadd
            m_old = m_ref[p]
            m_new = jnp.maximum(m_old, jnp.max(x, axis=1, keepdims=True))
            m_safe = jnp.where(m_new == NEG_INF, 0.0, m_new)
            alpha = jnp.exp(m_old - m_safe)
            pe = jnp.exp(x - m_safe)
            l_ref[p] = alpha * l_ref[p] + jnp.sum(pe, axis=1, keepdims=True)
            m_ref[p] = m_new
            pv = _dot(_mxu(pe), _mxu(vp[:, p * LANES:(p + 1) * LANES]))
            acc_ref[p] = alpha * acc_ref[p] + pv

    for r in range(PAGES_PER_STEP):
        s_idx = (j * PAGES_PER_STEP + r) * PAGE + lax.broadcasted_iota(jnp.int32, (1, PAGE), 1)
        bucket = _t5_bucket(past - s_idx)
        bias8 = jnp.zeros((A_HEADS, PAGE), F32)
        for b in range(N_BUCKETS):
            bias8 = jnp.where(bucket == b, tb_ref[b], bias8)
        valid = _in_topk(sc_ref[0, :, r * PAGE:(r + 1) * PAGE], s_idx, thr, cut)
        step(k_refs[r][0], v_refs[r][0], valid, bias8)

    @pl.when(j == nj - 1)
    def _():
        valid = (_in_topk(sn_ref[0], past, thr, cut)
                 & (lax.broadcasted_iota(jnp.int32, (1, LANES), 1) == 0))
        kn = jnp.broadcast_to(kn_ref[0], (LANES, D_KV))
        vn = jnp.broadcast_to(vn_ref[0], (LANES, D_KV))
        step(kn, vn, valid, tb_ref[0])
        for p in range(2):
            out = acc_ref[p] / l_ref[p]
            o_ref[0, :, (2 * p) * LANES:(2 * p + 1) * LANES] = jnp.where(lo, out[0:1], out[2:3])
            o_ref[0, :, (2 * p + 1) * LANES:(2 * p + 2) * LANES] = jnp.where(lo, out[1:2], out[3:4])


def _sample_attention(page_table, tab_b, qs, scores, thr, cut, s_new, k_new, v_new,
                      cache_k, cache_v):
    db, n_pages = page_table.shape
    npg = n_pages // PAGES_PER_STEP
    page_spec = lambda r: pl.BlockSpec(
        (1, PAGE, D_KV), lambda b, j, pt: (pt[b, j * PAGES_PER_STEP + r], 0, 0))
    per_seq = lambda shape: pl.BlockSpec((1,) + shape, lambda b, j, pt: (b,) + (0,) * len(shape))
    gs = pltpu.PrefetchScalarGridSpec(
        num_scalar_prefetch=1,
        grid=(db, npg),
        in_specs=[pl.BlockSpec(tab_b.shape, lambda b, j, pt: (0, 0, 0)),
                  per_seq((2, A_HEADS, LANES)),
                  pl.BlockSpec((1, 1, PAGES_PER_STEP * PAGE), lambda b, j, pt: (b, 0, j)),
                  per_seq((1, LANES)), per_seq((1, LANES)), per_seq((1, LANES)),
                  per_seq((1, D_KV)), per_seq((1, D_KV))]
        + [page_spec(r) for r in range(PAGES_PER_STEP)] * 2,
        out_specs=per_seq((1, D_ATTN)),
        scratch_shapes=[pltpu.VMEM((2, A_HEADS, 1), F32),
                        pltpu.VMEM((2, A_HEADS, 1), F32),
                        pltpu.VMEM((2, A_HEADS, LANES), F32)],
    )
    return pl.pallas_call(
        _sample_attn_kernel,
        grid_spec=gs,
        out_shape=jax.ShapeDtypeStruct((db, 1, D_ATTN), F32),
        compiler_params=pltpu.CompilerParams(
            dimension_semantics=("arbitrary", "arbitrary"), vmem_limit_bytes=VMEM_LIMIT),
        name="sample_attn",
    )(page_table, tab_b, qs, scores, thr, cut, s_new, k_new, v_new,
      *([cache_k] * PAGES_PER_STEP), *([cache_v] * PAGES_PER_STEP))


def _permute_heads(a, axis):
    shape = a.shape
    a = a.reshape(shape[:axis] + (A_HEADS, HEAD_DIM) + shape[axis + 1:])
    a = jnp.take(a, jnp.array(HEAD_PERM), axis=axis)
    return a.reshape(shape)


def _pack_w_in(w):
    d = w.shape[0]
    o = [0]
    for wd in (D_ATTN, D_KV, D_KV, D_ATTN, IDX_HEADS * IDX_DIM, IDX_HEADS, IDX_DIM,
               D_H, D_H, D_H, D_H):
        o.append(o[-1] + wd)
    a_q, a_k, a_v, a_g, i_q, i_w, i_k, h_q, h_f, h_i, h_g = (
        w[:, o[n]:o[n + 1]] for n in range(11))
    z = lambda n: jnp.zeros((d, n), w.dtype)
    packed = jnp.concatenate(
        [_permute_heads(a_q, 1), a_k, a_v, _permute_heads(a_g, 1), i_q, i_k, i_k,
         z(IDX_DIM), i_w, z(LANES - IDX_DIM - IDX_HEADS), h_q, h_f, h_i, h_g], axis=1)
    assert packed.shape[1] == C_END
    return packed.astype(MXU_DTYPE)


def kernel(x_prompt, x_sample, cache_k, cache_v, cache_ik, state_hgrn, page_table, rel_bias,
           ln_g, w_in, q_norm_g, k_norm_g, hgrn_lb, hgrn_norm_g, w_out):
    b, s, d = x_prompt.shape
    db, dt, _ = x_sample.shape
    depth, n_pool = cache_k.shape[:2]
    n_pages = page_table.shape[1]
    past = n_pages * PAGE
    assert depth == 1 and dt == 1 and hgrn_lb.shape[0] == 2
    assert s % HG_BLK == 0 and s >= 4 * TOPK and past >= 4 * TOPK
    assert n_pages % PAGES_PER_STEP == 0 and past % K_CHUNK == 0 and db % 8 == 0

    w_pack = _pack_w_in(w_in[0])
    w_out_pack = jnp.concatenate(
        [_permute_heads(w_out[0][:D_ATTN], 0), w_out[0][D_ATTN:]], axis=0).astype(MXU_DTYPE)
    qg = jnp.tile(q_norm_g[0], A_HEADS)[None]
    kg = jnp.tile(k_norm_g[0], A_KV_HEADS)[None]
    hng = jnp.tile(hgrn_norm_g[0], H_HEADS)[None]
    bd = _block_diag(D_ATTN, HEAD_DIM, 1.0 / HEAD_DIM, MXU_DTYPE)
    lng = ln_g[0][None]

    pp = _project(x_prompt.reshape(b * s, d), lng, w_pack, qg, kg, hgrn_lb, bd, 256)
    r3 = lambda a: a.reshape(b, s, a.shape[-1])
    lp = -(-(s + Q_BLK) // K_CHUNK) * K_CHUNK
    padk = lambda a: jnp.pad(r3(a), ((0, 0), (Q_BLK, lp - s - Q_BLK), (0, 0)))
    iw_p = r3(pp["ikw"])[:, :, IDX_DIM:IDX_DIM + IDX_HEADS]
    a_out = _attn_prompt(rel_bias, _bias_tiles(rel_bias), r3(pp["q"]), r3(pp["iq"]), iw_p,
                         padk(pp["kb"]), padk(pp["vb"]), padk(pp["ik2"]))
    h_out, st = _hgrn_prompt(r3(pp["hq"]), r3(pp["hg"]), r3(pp["hk"]), r3(pp["hv"]))
    y_prompt = _merge(x_prompt.reshape(b * s, d), a_out.reshape(b * s, D_ATTN), pp["sga"],
                      h_out.reshape(b * s, D_H), pp["sgh"], hng, bd, w_out_pack, 256)
    st = st.reshape(b, D_H // LANES, 2, H_KEY, 2, H_KEY)
    s_prompt = jnp.stack([st[:, :, e, :, e, :] for e in range(2)], axis=2)
    s_prompt = s_prompt.reshape(b, H_HEADS, H_KEY, H_KEY).transpose(0, 1, 3, 2)

    sp = _project(x_sample.reshape(db, d), lng, w_pack, qg, kg, hgrn_lb, bd, db)
    ik_s = sp["ikw"][:, :IDX_DIM]
    iw_s = sp["ikw"][:, IDX_DIM:IDX_DIM + IDX_HEADS]
    scores, s_new = _sample_scores(
        page_table, sp["iq"].reshape(db, IDX_HEADS, IDX_DIM), iw_s.reshape(db, IDX_HEADS, 1),
        ik_s.reshape(db, 1, IDX_DIM), cache_ik)
    thr, cut = _sample_threshold(scores.reshape(db, past), s_new.reshape(db, LANES))
    qg4 = sp["q"].reshape(db, 2, 2, 2, HEAD_DIM)
    rows = []
    for p in range(2):
        g0lo, g0hi = qg4[:, p, 0, 0], qg4[:, p, 0, 1]
        g1lo, g1hi = qg4[:, p, 1, 0], qg4[:, p, 1, 1]
        z1 = jnp.zeros_like(g0lo)
        pr = jnp.stack([jnp.concatenate([g0lo, z1], -1), jnp.concatenate([g1lo, z1], -1),
                        jnp.concatenate([z1, g0hi], -1), jnp.concatenate([z1, g1hi], -1)]
                       + [jnp.zeros((db, LANES), sp["q"].dtype)] * 4, axis=1)
        rows.append(pr)
    qs = jnp.stack(rows, axis=1)
    tab_b = jnp.broadcast_to(rel_bias[:, :, None], (N_BUCKETS, A_HEADS, LANES))
    a_out_s = _sample_attention(
        page_table, tab_b, qs, scores, thr.reshape(db, 1, LANES), cut.reshape(db, 1, LANES), s_new,
        sp["kf"].reshape(db, 1, D_KV), sp["vf"].reshape(db, 1, D_KV),
        cache_k.reshape(n_pool, PAGE, D_KV), cache_v.reshape(n_pool, PAGE, D_KV))
    h_out_s, s_sample = _hgrn_sample(sp["hq"], sp["hg"], sp["hk"], sp["hv"], state_hgrn[0])
    y_sample = _merge(x_sample.reshape(db, d), a_out_s.reshape(db, D_ATTN), sp["sga"],
                      h_out_s.reshape(db, D_H), sp["sgh"], hng, bd, w_out_pack, db)

    kv5 = lambda a, n: a.reshape(1, n, -1, A_KV_HEADS, HEAD_DIM)
    return (y_prompt.reshape(b, s, d), y_sample.reshape(db, 1, d),
            kv5(pp["kf"], b), kv5(pp["vf"], b),
            pp["ikw"][:, :IDX_DIM].reshape(1, b, s, IDX_DIM), s_prompt[None],
            kv5(sp["kf"], db), kv5(sp["vf"], db),
            ik_s.reshape(1, db, 1, IDX_DIM), s_sample[None])
```

```python
import functools
import math

import jax
import jax.numpy as jnp
from jax import lax
from jax.experimental import pallas as pl
from jax.experimental.pallas import tpu as pltpu

F32 = jnp.float32
BF16 = jnp.bfloat16
MXU_DTYPE = BF16

HEAD_DIM = 64
A_HEADS = 8
A_KV_HEADS = 4
D_ATTN = A_HEADS * HEAD_DIM
D_KV = A_KV_HEADS * HEAD_DIM
IDX_HEADS = 8
IDX_DIM = 64
H_HEADS = 8
H_KEY = 64
D_H = H_HEADS * H_KEY
TOPK = 256
PAGE = 128
N_BUCKETS = 32
MAX_DISTANCE = 128
EPS = 1e-6
LANES = 128
Q_BLK = 128
K_CHUNK = 512
HG_BLK = 256
HG_CHUNK = 16
HG_SUB = 32
MAX_PAGES_PER_STEP = 32
VMEM_LIMIT = 48 * 1024 * 1024
NEG_INF = float("-inf")
LOG2E = math.log2(math.e)
KEY_NEG_INF = -2139095041

C_AQ, C_AK, C_AV, C_AG, C_IQ, C_IK, C_IW, C_HQ, C_HF, C_HI, C_HG, C_END = (
    0, 512, 768, 1024, 1536, 2048, 2176, 2304, 2816, 3328, 3840, 4352)
HEAD_PERM = (0, 2, 1, 3, 4, 6, 5, 7)


def _mxu(x):
    return x.astype(MXU_DTYPE)


def _dot(a, b):
    return jnp.dot(a, b, preferred_element_type=F32)


def _dot_nt(a, b):
    return lax.dot_general(a, b, (((1,), (1,)), ((), ())), preferred_element_type=F32)


def _dot_tn(a, b):
    return lax.dot_general(a, b, (((0,), (0,)), ((), ())), preferred_element_type=F32)


def _split_dot(x, w, parts):
    acc = None
    r = x
    for _ in range(parts):
        p = _mxu(r)
        t = _dot(p, w)
        acc = t if acc is None else acc + t
        r = r - p.astype(F32)
    return acc


def _sigmoid(x):
    return 1.0 / (1.0 + jnp.exp(-x))


def _silu(x):
    return x * _sigmoid(x)


def _t5_bucket(n):
    n = jnp.maximum(n, 0)
    max_exact = N_BUCKETS // 2
    nf = jnp.maximum(n, 1).astype(F32)
    large = max_exact + jnp.floor(jnp.log(nf / max_exact) / math.log(MAX_DISTANCE / max_exact)
                                  * (N_BUCKETS - max_exact)).astype(jnp.int32)
    large = jnp.minimum(large, N_BUCKETS - 1)
    return jnp.where(n < max_exact, n, large)


def _block_diag(n, blk, val, dtype):
    r = lax.broadcasted_iota(jnp.int32, (n, n), 0) // blk
    c = lax.broadcasted_iota(jnp.int32, (n, n), 1) // blk
    return jnp.where(r == c, val, 0.0).astype(dtype)


def _proj_kernel(x_ref, lng_ref, w_ref, qg_ref, kg_ref, lb_ref, bd_ref,
                 q_ref, iq_ref, kf_ref, kb_ref, vf_ref, vb_ref, ikw_ref, ik2_ref,
                 sga_ref, hq_ref, hg_ref, hk_ref, hv_ref, sgh_ref):
    x = x_ref[...]
    ms = jnp.mean(x * x, axis=-1, keepdims=True)
    xb = _mxu(x * lax.rsqrt(ms + EPS) * lng_ref[...])

    def seg(a, b):
        return _dot(xb, w_ref[:, a:b])

    bd = bd_ref[...]

    aq = seg(C_AQ, C_AK)
    msq = _split_dot(aq * aq, bd, 2)
    q_ref[...] = (aq * lax.rsqrt(msq + EPS) * qg_ref[...]
                  * (HEAD_DIM ** -0.5 * LOG2E)).astype(q_ref.dtype)

    ak = seg(C_AK, C_AV)
    msk = _split_dot(ak * ak, bd[:D_KV, :D_KV], 2)
    k = ak * lax.rsqrt(msk + EPS) * kg_ref[...]
    kf_ref[...] = k
    kb_ref[...] = k.astype(kb_ref.dtype)

    v = seg(C_AV, C_AG)
    vf_ref[...] = v
    vb_ref[...] = v.astype(vb_ref.dtype)

    sga_ref[...] = _silu(seg(C_AG, C_IQ))
    iq_ref[...] = seg(C_IQ, C_IK).astype(iq_ref.dtype)

    ikk = seg(C_IK, C_IW)
    iww = seg(C_IW, C_HQ) * (IDX_HEADS ** -0.5 * IDX_DIM ** -0.5)
    ik2_ref[...] = ikk.astype(ik2_ref.dtype)
    lane = lax.broadcasted_iota(jnp.int32, ikk.shape, 1)
    ikw_ref[...] = jnp.where(lane < IDX_DIM, ikk, iww)

    hq_ref[...] = _silu(seg(C_HQ, C_HF)) * H_KEY ** -0.5
    lbp = lb_ref[...]
    mx = jnp.max(lbp, axis=0, keepdims=True)
    e = jnp.exp(lbp - mx)
    lb = e[0:1] / jnp.sum(e, axis=0, keepdims=True)
    f = lb + (1.0 - lb) * _sigmoid(seg(C_HF, C_HI))
    hg_ref[...] = jnp.log(f)
    hk_ref[...] = 1.0 - f
    hv_ref[...] = seg(C_HI, C_HG)
    sgh_ref[...] = _silu(seg(C_HG, C_END))


def _project(x2d, ln_g, w_pack, qg, kg, lbp, bd, tm):
    n, d = x2d.shape
    row = lambda w: pl.BlockSpec((tm, w), lambda i: (i, 0))
    full = lambda a: pl.BlockSpec(a.shape, lambda i: (0,) * a.ndim)
    outs = [("q", D_ATTN, MXU_DTYPE), ("iq", D_ATTN, MXU_DTYPE), ("kf", D_KV, F32),
            ("kb", D_KV, MXU_DTYPE), ("vf", D_KV, F32), ("vb", D_KV, MXU_DTYPE),
            ("ikw", LANES, F32), ("ik2", LANES, MXU_DTYPE), ("sga", D_ATTN, F32),
            ("hq", D_H, F32), ("hg", D_H, F32), ("hk", D_H, F32), ("hv", D_H, F32),
            ("sgh", D_H, F32)]
    res = pl.pallas_call(
        _proj_kernel,
        grid=(n // tm,),
        in_specs=[row(d), full(ln_g), full(w_pack), full(qg), full(kg), full(lbp), full(bd)],
        out_specs=[row(w) for _, w, _ in outs],
        out_shape=[jax.ShapeDtypeStruct((n, w), dt) for _, w, dt in outs],
        compiler_params=pltpu.CompilerParams(
            dimension_semantics=("arbitrary",), vmem_limit_bytes=VMEM_LIMIT),
        name="proj",
    )(x2d, ln_g, w_pack, qg, kg, lbp, bd)
    return dict(zip([o[0] for o in outs], res))


def _bias_tile_kernel(tab_ref, o_ref):
    j = lax.broadcasted_iota(jnp.int32, (2 * Q_BLK, Q_BLK), 0)
    r = lax.broadcasted_iota(jnp.int32, (2 * Q_BLK, Q_BLK), 1)
    bucket = _t5_bucket(Q_BLK + r - j)
    for h in range(A_HEADS):
        acc = jnp.zeros((2 * Q_BLK, Q_BLK), F32)
        for b in range(N_BUCKETS):
            acc = jnp.where(bucket == b, tab_ref[b, h] * LOG2E, acc)
        o_ref[h] = acc


def _bias_tiles(rel_bias):
    return pl.pallas_call(
        _bias_tile_kernel,
        in_specs=[pl.BlockSpec(memory_space=pltpu.SMEM)],
        out_shape=jax.ShapeDtypeStruct((A_HEADS, 2 * Q_BLK, Q_BLK), F32),
        name="bias_tiles",
    )(rel_bias)


def _key_to_f32(key):
    bits = key ^ ((key >> 31) & jnp.int32(0x7FFFFFFF))
    return lax.bitcast_convert_type(bits, F32)


def _topk_cut(sc_ref, nch, cut_ref, extra=None):
    ncol = sc_ref.shape[1]
    nrows = nch * K_CHUNK
    nbits = int(sc_ref.shape[0]).bit_length()
    groups = K_CHUNK // 8

    def count(pred):
        def body(c, acc):
            r0 = pl.multiple_of(c * K_CHUNK, K_CHUNK)
            hit = jnp.where(pred(sc_ref[pl.ds(r0, K_CHUNK), :], r0), 1, 0).astype(jnp.int32)
            return acc + jnp.sum(hit.reshape(groups, 8, ncol), axis=0, dtype=jnp.int32)

        acc = lax.fori_loop(0, nch, body, jnp.zeros((8, ncol), jnp.int32))
        cnt = jnp.sum(acc, axis=0, keepdims=True, dtype=jnp.int32)
        if extra is not None:
            cnt = cnt + jnp.where(pred(extra, nrows), 1, 0).astype(jnp.int32)
        return cnt

    def bit_body(it, carry):
        key, n_ge = carry
        cand = key + jnp.left_shift(jnp.int32(1), 31 - it)
        cf = _key_to_f32(cand)
        cnt = count(lambda v, r0: v >= cf)
        ok = cnt >= TOPK
        return jnp.where(ok, cand, key), jnp.where(ok, cnt, n_ge)

    key0 = jnp.full((1, ncol), jnp.iinfo(jnp.int32).min, jnp.int32)
    key, n_ge = lax.fori_loop(0, 32, bit_body, (key0, jnp.full((1, ncol), TOPK, jnp.int32)))
    below = key < KEY_NEG_INF
    thr = _key_to_f32(jnp.maximum(key, KEY_NEG_INF))
    cut_ref[...] = jnp.full((1, ncol), jnp.iinfo(jnp.int32).max, jnp.int32)

    @pl.when(jnp.max(jnp.where(below, TOPK + 1, n_ge)) > TOPK)
    def _():
        need = TOPK - count(lambda v, r0: v > thr)

        def row_of(v, r0):
            return r0 + lax.broadcasted_iota(jnp.int32, v.shape, 0)

        def idx_body(it, x):
            cand = x + jnp.left_shift(jnp.int32(1), nbits - 1 - it)
            g = count(lambda v, r0: (v == thr) & (row_of(v, r0) < cand))
            return jnp.where(g < need, cand, x)

        cut_ref[...] = lax.fori_loop(0, nbits, idx_body, jnp.zeros((1, ncol), jnp.int32))

    return thr


def _in_topk(sc, col, thr, cut):
    return (sc > thr) | ((sc == thr) & (col <= cut))


def _row_halves(x):
    half = x.shape[0] // 2
    zero = jnp.zeros((half, x.shape[1]), x.dtype)
    return (jnp.concatenate([x[:half], zero], axis=0), jnp.concatenate([zero, x[half:]], axis=0))


def _attn_prompt_kernel(tab_ref, qt_ref, iqt_ref, iwt_ref, kb_ref, vt_ref, ik2_ref, bt_ref,
                        o_ref, sc_ref, cut_ref, m_ref, l_ref, acc_ref):
    i = pl.program_id(1)
    t_pos = i * Q_BLK + lax.broadcasted_iota(jnp.int32, (1, Q_BLK), 1)

    iqt = iqt_ref[0]
    cols = []
    for g in range(D_ATTN // LANES):
        cols.extend(_row_halves(iqt[g * LANES:(g + 1) * LANES]))
    iq_stack = jnp.concatenate(cols, axis=1)
    iw = iwt_ref[0]
    nch = (i + 5) // 4

    def score_body(c, carry):
        r0 = pl.multiple_of(c * K_CHUNK, K_CHUNK)
        s_all = _dot(ik2_ref[0, pl.ds(r0, K_CHUNK), :], iq_stack)
        sc = jnp.zeros((K_CHUNK, Q_BLK), F32)
        for h in range(IDX_HEADS):
            sc = sc + jnp.maximum(s_all[:, h * Q_BLK:(h + 1) * Q_BLK], 0.0) * iw[h:h + 1]
        s_glob = r0 - Q_BLK + lax.broadcasted_iota(jnp.int32, (K_CHUNK, Q_BLK), 0)
        valid = (s_glob >= 0) & (s_glob <= t_pos)
        sc_ref[pl.ds(r0, K_CHUNK), :] = jnp.where(valid, sc, NEG_INF)
        return carry

    lax.fori_loop(0, nch, score_body, 0)
    thr = _topk_cut(sc_ref, nch, cut_ref)
    cut = cut_ref[...]

    m_ref[...] = jnp.full(m_ref.shape, NEG_INF, F32)
    l_ref[...] = jnp.zeros(l_ref.shape, F32)
    acc_ref[...] = jnp.zeros(acc_ref.shape, F32)

    qt = qt_ref[0]
    q_pairs = []
    for p in range(2):
        g0 = _row_halves(qt[(2 * p) * LANES:(2 * p + 1) * LANES])
        g1 = _row_halves(qt[(2 * p + 1) * LANES:(2 * p + 2) * LANES])
        q_pairs.append(jnp.concatenate([g0[0], g1[0], g0[1], g1[1]], axis=1))

    def attend(r0, width, valid, bias_fn, const_fn):
        for p in range(2):
            kc = kb_ref[0, pl.ds(r0, width), p * LANES:(p + 1) * LANES]
            vc = vt_ref[0, p * LANES:(p + 1) * LANES, pl.ds(r0, width)]
            st = _dot(kc, q_pairs[p])
            ps, alphas = [], []
            for a in range(4):
                h = 4 * p + a
                x = st[:, a * Q_BLK:(a + 1) * Q_BLK]
                bias = bias_fn(h)
                if bias is not None:
                    x = x + bias
                x = jnp.where(valid, x, NEG_INF)
                c = const_fn(h)
                m_old = m_ref[h:h + 1]
                m_new = jnp.maximum(m_old, jnp.max(x, axis=0, keepdims=True) + c)
                m_safe = jnp.where(m_new == NEG_INF, 0.0, m_new)
                alpha = jnp.exp2(m_old - m_safe)
                pe = jnp.exp2(x - (m_safe - c))
                l_ref[h:h + 1] = alpha * l_ref[h:h + 1] + jnp.sum(pe, axis=0, keepdims=True)
                m_ref[h:h + 1] = m_new
                ps.append(_mxu(pe))
                alphas.append(alpha)
            pv = _dot(vc, jnp.concatenate(ps, axis=1))
            acc_ref[p] = acc_ref[p] * jnp.concatenate(alphas, axis=1) + pv

    def far_body(c, carry):
        r0 = pl.multiple_of(c * K_CHUNK, K_CHUNK)
        row = r0 + lax.broadcasted_iota(jnp.int32, (K_CHUNK, Q_BLK), 0)
        valid = ((row >= Q_BLK) & (row < i * Q_BLK)
                 & _in_topk(sc_ref[pl.ds(r0, K_CHUNK), :], row, thr, cut))
        attend(r0, K_CHUNK, valid, lambda h: None, lambda h: tab_ref[N_BUCKETS - 1, h] * LOG2E)
        return carry

    lax.fori_loop(0, (i + 3) // 4, far_body, 0)

    r0 = pl.multiple_of(i * Q_BLK, Q_BLK)
    row = r0 + lax.broadcasted_iota(jnp.int32, (2 * Q_BLK, Q_BLK), 0)
    valid = ((row >= Q_BLK) & (row - Q_BLK <= t_pos)
             & _in_topk(sc_ref[pl.ds(r0, 2 * Q_BLK), :], row, thr, cut))
    attend(r0, 2 * Q_BLK, valid, lambda h: bt_ref[h], lambda h: 0.0)

    top = lax.broadcasted_iota(jnp.int32, (LANES, Q_BLK), 0) < HEAD_DIM
    for p in range(2):
        acc = acc_ref[p]
        outs = [acc[:, a * Q_BLK:(a + 1) * Q_BLK] / l_ref[4 * p + a:4 * p + a + 1] for a in range(4)]
        o_ref[0, :, (2 * p) * LANES:(2 * p + 1) * LANES] = jnp.where(top, outs[0], outs[2]).T
        o_ref[0, :, (2 * p + 1) * LANES:(2 * p + 2) * LANES] = jnp.where(top, outs[1], outs[3]).T


def _attn_prompt(rel_bias, bias_tiles, qt, iqt, iwt, kbp, vtp, ik2p):
    b, _, s = qt.shape
    lp = kbp.shape[1]
    blk_t = lambda r: pl.BlockSpec((1, r, Q_BLK), lambda bi, i: (bi, 0, i))
    return pl.pallas_call(
        _attn_prompt_kernel,
        grid=(b, s // Q_BLK),
        in_specs=[pl.BlockSpec(memory_space=pltpu.SMEM),
                  blk_t(D_ATTN), blk_t(D_ATTN), blk_t(IDX_HEADS),
                  pl.BlockSpec((1, lp, D_KV), lambda bi, i: (bi, 0, 0)),
                  pl.BlockSpec((1, D_KV, lp), lambda bi, i: (bi, 0, 0)),
                  pl.BlockSpec((1, lp, LANES), lambda bi, i: (bi, 0, 0)),
                  pl.BlockSpec(bias_tiles.shape, lambda bi, i: (0, 0, 0))],
        out_specs=pl.BlockSpec((1, Q_BLK, D_ATTN), lambda bi, i: (bi, i, 0)),
        out_shape=jax.ShapeDtypeStruct((b, s, D_ATTN), F32),
        scratch_shapes=[pltpu.VMEM((lp, Q_BLK), F32),
                        pltpu.VMEM((1, Q_BLK), jnp.int32),
                        pltpu.VMEM((A_HEADS, Q_BLK), F32),
                        pltpu.VMEM((A_HEADS, Q_BLK), F32),
                        pltpu.VMEM((2, LANES, 4 * Q_BLK), F32)],
        compiler_params=pltpu.CompilerParams(
            dimension_semantics=("arbitrary", "arbitrary"), vmem_limit_bytes=VMEM_LIMIT),
        name="attn_prompt",
    )(rel_bias, qt, iqt, iwt, kbp, vtp, ik2p, bias_tiles)


def _hgrn_prompt_kernel(q_ref, g_ref, k_ref, v_ref, o_ref, st_ref,
                        kp_ref, bp_ref, vp_ref, oi_ref):
    j = pl.program_id(1)
    n_pairs = D_H // LANES

    @pl.when(j == 0)
    def _():
        st_ref[...] = jnp.zeros(st_ref.shape, F32)

    q = q_ref[0]
    k = k_ref[0]
    v = v_ref[0]
    r = lax.broadcasted_iota(jnp.int32, (HG_BLK, HG_BLK), 0)
    c = lax.broadcasted_iota(jnp.int32, (HG_BLK, HG_BLK), 1)
    tri = jnp.where(c <= r, 1.0, 0.0).astype(MXU_DTYPE)
    bcum = _split_dot_left(tri, g_ref[0], 3)

    pad = HG_CHUNK
    zeros_pad = jnp.zeros((pad, D_H), F32)
    kp_ref[0:pad, :] = zeros_pad
    bp_ref[0:pad, :] = zeros_pad
    vp_ref[0:pad, :] = zeros_pad
    kp_ref[pad:pad + HG_BLK, :] = k
    bp_ref[pad:pad + HG_BLK, :] = bcum
    vp_ref[pad:pad + HG_BLK, :] = v

    bd1 = _block_diag(D_H, H_KEY, 1.0, MXU_DTYPE)
    for sb in range(HG_BLK // HG_SUB):
        r0 = sb * HG_SUB
        qs = q[r0:r0 + HG_SUB]
        bs = bcum[r0:r0 + HG_SUB]
        tin = lax.broadcasted_iota(jnp.int32, (HG_SUB, D_H), 0) % HG_CHUNK
        acc = jnp.zeros((HG_SUB, D_H), F32)
        for d in range(HG_CHUNK):
            a = pad + r0 - d
            kd = kp_ref[a:a + HG_SUB, :]
            vd = vp_ref[a:a + HG_SUB, :]
            if d == 0:
                pr = qs * kd
            else:
                bd_ = bp_ref[a:a + HG_SUB, :]
                pr = jnp.where(tin >= d, qs, 0.0) * kd * jnp.exp(bs - bd_)
            acc = acc + _dot(_mxu(pr), bd1) * vd
        oi_ref[r0:r0 + HG_SUB, :] = acc

    lane_r = lax.broadcasted_iota(jnp.int32, (LANES, LANES), 0) // H_KEY
    lane_c = lax.broadcasted_iota(jnp.int32, (LANES, LANES), 1) // H_KEY
    same_head = lane_r == lane_c
    for cch in range(HG_BLK // HG_CHUNK):
        r0 = cch * HG_CHUNK
        bch = bcum[r0:r0 + HG_CHUNK]
        b0 = bp_ref[pad + r0 - 1:pad + r0, :]
        bl = bp_ref[pad + r0 + HG_CHUNK - 1:pad + r0 + HG_CHUNK, :]
        qi = _mxu(q[r0:r0 + HG_CHUNK] * jnp.exp(bch - b0))
        kl = _mxu(k[r0:r0 + HG_CHUNK] * jnp.exp(bl - bch))
        vch = _mxu(v[r0:r0 + HG_CHUNK])
        dec = jnp.exp(bl - b0)
        outs = []
        for p in range(n_pairs):
            sl = slice(p * LANES, (p + 1) * LANES)
            st = st_ref[0, p]
            outs.append(_dot_nt(qi[:, sl], _mxu(st)))
            ds = _dot_tn(vch[:, sl], kl[:, sl])
            st_ref[0, p] = st * dec[:, sl] + jnp.where(same_head, ds, 0.0)
        o_ref[0, r0:r0 + HG_CHUNK, :] = oi_ref[r0:r0 + HG_CHUNK, :] + jnp.concatenate(outs, axis=1)


def _split_dot_left(w, x, parts):
    acc = None
    r = x
    for _ in range(parts):
        p = _mxu(r)
        t = _dot(w, p)
        acc = t if acc is None else acc + t
        r = r - p.astype(F32)
    return acc


def _hgrn_prompt(hq, hg, hk, hv):
    b, s, _ = hq.shape
    blk = pl.BlockSpec((1, HG_BLK, D_H), lambda bi, j: (bi, j, 0))
    n_pairs = D_H // LANES
    return pl.pallas_call(
        _hgrn_prompt_kernel,
        grid=(b, s // HG_BLK),
        in_specs=[blk, blk, blk, blk],
        out_specs=[blk, pl.BlockSpec((1, n_pairs, LANES, LANES), lambda bi, j: (bi, 0, 0, 0))],
        out_shape=[jax.ShapeDtypeStruct((b, s, D_H), F32),
                   jax.ShapeDtypeStruct((b, n_pairs, LANES, LANES), F32)],
        scratch_shapes=[pltpu.VMEM((HG_CHUNK + HG_BLK, D_H), F32)] * 3
        + [pltpu.VMEM((HG_BLK, D_H), F32)],
        compiler_params=pltpu.CompilerParams(
            dimension_semantics=("arbitrary", "arbitrary"), vmem_limit_bytes=VMEM_LIMIT),
        name="hgrn_prompt",
    )(hq, hg, hk, hv)


def _hgrn_sample_kernel(q_ref, g_ref, k_ref, v_ref, s_ref, o_ref, sn_ref):
    q = q_ref[...]
    e = jnp.exp(g_ref[...])
    k = k_ref[...]
    v = v_ref[...]
    o = jnp.zeros(v.shape, F32)
    for kk in range(H_KEY):
        sn = e[kk:kk + 1] * s_ref[0, kk] + k[kk:kk + 1] * v
        sn_ref[0, kk] = sn
        o = o + q[kk:kk + 1] * sn
    o_ref[...] = o


def _hgrn_sample(hq, hg, hk, hv, state_t):
    db = hq.shape[0]
    hspec = pl.BlockSpec((H_KEY, db), lambda h: (h, 0))
    sspec = pl.BlockSpec((1, H_KEY, H_KEY, db), lambda h: (h, 0, 0, 0))
    o_t, s_new = pl.pallas_call(
        _hgrn_sample_kernel,
        grid=(H_HEADS,),
        in_specs=[hspec, hspec, hspec, hspec, sspec],
        out_specs=[hspec, sspec],
        out_shape=[jax.ShapeDtypeStruct((D_H, db), F32),
                   jax.ShapeDtypeStruct(state_t.shape, F32)],
        compiler_params=pltpu.CompilerParams(
            dimension_semantics=("arbitrary",), vmem_limit_bytes=VMEM_LIMIT),
        name="hgrn_sample",
    )(hq.T, hg.T, hk.T, hv.T, state_t)
    return o_t.T, s_new


def _merge_kernel(x_ref, a_ref, sga_ref, h_ref, sgh_ref, hng_ref, bd_ref, w_ref, y_ref):
    h = h_ref[...]
    msq = _split_dot(h * h, bd_ref[...], 2)
    hn = h * lax.rsqrt(msq + EPS) * hng_ref[...]
    ma = _mxu(a_ref[...] * sga_ref[...])
    mh = _mxu(hn * sgh_ref[...])
    y_ref[...] = x_ref[...] + _dot(ma, w_ref[0:D_ATTN, :]) + _dot(mh, w_ref[D_ATTN:, :])


def _merge(x2d, a_out, sga, h_out, sgh, hng, bd, w_pack, tm):
    n, d = x2d.shape
    row = lambda w: pl.BlockSpec((tm, w), lambda i: (i, 0))
    full = lambda a: pl.BlockSpec(a.shape, lambda i: (0,) * a.ndim)
    return pl.pallas_call(
        _merge_kernel,
        grid=(n // tm,),
        in_specs=[row(d), row(D_ATTN), row(D_ATTN), row(D_H), row(D_H),
                  full(hng), full(bd), full(w_pack)],
        out_specs=row(d),
        out_shape=jax.ShapeDtypeStruct((n, d), F32),
        compiler_params=pltpu.CompilerParams(
            dimension_semantics=("arbitrary",), vmem_limit_bytes=VMEM_LIMIT),
        name="merge",
    )(x2d, a_out, sga, h_out, sgh, hng, bd, w_pack)


def _sample_score_kernel(pps, pt_ref, iqh_ref, iwc_ref, ikn_ref, *refs):
    ik_refs = refs[:pps]
    sc_ref, sn_ref = refs[pps:]
    iqh = iqh_ref[0]
    iwc = iwc_ref[0]
    ikt = jnp.concatenate([_mxu(r[0]) for r in ik_refs], axis=1)
    s = _dot(iqh, ikt)
    sc_ref[0] = jnp.sum(jnp.maximum(s, 0.0) * iwc, axis=0, keepdims=True)
    prod = iqh.astype(F32) * _mxu(ikn_ref[0]).astype(F32)
    s_new = jnp.sum(prod, axis=1, keepdims=True)
    s_new = jnp.sum(jnp.maximum(s_new, 0.0) * iwc, axis=0, keepdims=True)
    sn_ref[0] = jnp.broadcast_to(s_new, (1, LANES))


def _sample_scores(page_table, iqh, iwc, ikn, cache_ikt, pps):
    db, n_pages = page_table.shape
    page_spec = lambda r: pl.BlockSpec(
        (1, IDX_DIM, PAGE), lambda b, j, pt: (pt[b, j * pps + r], 0, 0))
    gs = pltpu.PrefetchScalarGridSpec(
        num_scalar_prefetch=1,
        grid=(db, n_pages // pps),
        in_specs=[pl.BlockSpec((1, IDX_HEADS, IDX_DIM), lambda b, j, pt: (b, 0, 0)),
                  pl.BlockSpec((1, IDX_HEADS, 1), lambda b, j, pt: (b, 0, 0)),
                  pl.BlockSpec((1, 1, IDX_DIM), lambda b, j, pt: (b, 0, 0))]
        + [page_spec(r) for r in range(pps)],
        out_specs=[pl.BlockSpec((1, 1, pps * PAGE), lambda b, j, pt: (b, 0, j)),
                   pl.BlockSpec((1, 1, LANES), lambda b, j, pt: (b, 0, 0))],
    )
    return pl.pallas_call(
        functools.partial(_sample_score_kernel, pps),
        grid_spec=gs,
        out_shape=[jax.ShapeDtypeStruct((db, 1, n_pages * PAGE), F32),
                   jax.ShapeDtypeStruct((db, 1, LANES), F32)],
        compiler_params=pltpu.CompilerParams(
            dimension_semantics=("arbitrary", "arbitrary"), vmem_limit_bytes=VMEM_LIMIT),
        name="sample_scores",
    )(page_table, iqh, iwc, ikn, *([cache_ikt] * pps))


def _sample_thr_kernel(sc_ref, sn_ref, thr_ref, cut_ref):
    past = sc_ref.shape[0]
    thr_ref[...] = _topk_cut(sc_ref, past // K_CHUNK, cut_ref, extra=sn_ref[...])


def _sample_threshold(scores_t, s_new_t):
    db = scores_t.shape[1]
    return pl.pallas_call(
        _sample_thr_kernel,
        out_shape=[jax.ShapeDtypeStruct((1, db), F32),
                   jax.ShapeDtypeStruct((1, db), jnp.int32)],
        compiler_params=pltpu.CompilerParams(vmem_limit_bytes=VMEM_LIMIT),
        name="sample_threshold",
    )(scores_t, s_new_t)


def _sample_attn_kernel(pps, pt_ref, tb_ref, qs_ref, sc_ref, thr_ref, cut_ref, sn_ref,
                        kn_ref, vn_ref, *refs):
    k_refs = refs[:pps]
    v_refs = refs[pps:2 * pps]
    o_ref, m_ref, l_ref, acc_ref = refs[2 * pps:]
    j = pl.program_id(1)
    nj = pl.num_programs(1)
    n = pps * PAGE
    past = nj * n

    @pl.when(j == 0)
    def _():
        m_ref[...] = jnp.full(m_ref.shape, NEG_INF, F32)
        l_ref[...] = jnp.zeros(l_ref.shape, F32)
        acc_ref[...] = jnp.zeros(acc_ref.shape, F32)

    thr = thr_ref[0, :, 0:1]
    cut = cut_ref[0, :, 0:1]
    qs = qs_ref[0]
    lo = lax.broadcasted_iota(jnp.int32, (1, LANES), 1) < HEAD_DIM

    def update(p, x, pv_fn):
        m_old = m_ref[p]
        m_new = jnp.maximum(m_old, jnp.max(x, axis=1, keepdims=True))
        m_safe = jnp.where(m_new == NEG_INF, 0.0, m_new)
        alpha = jnp.exp2(m_old - m_safe)
        pe = jnp.exp2(x - m_safe)
        l_ref[p] = alpha * l_ref[p] + jnp.sum(pe, axis=1, keepdims=True)
        m_ref[p] = m_new
        acc_ref[p] = alpha * acc_ref[p] + pv_fn(_mxu(pe))

    def head_rows(b8, p):
        return jnp.concatenate([b8[4 * p:4 * p + 4], jnp.zeros((4, b8.shape[1]), F32)], axis=0)

    s_idx = j * n + lax.broadcasted_iota(jnp.int32, (1, n), 1)
    s_last = s_idx[:, n - PAGE:]
    bucket = _t5_bucket(past - s_last)
    near = jnp.zeros((A_HEADS, PAGE), F32)
    for b in range(N_BUCKETS):
        near = jnp.where(bucket == b, tb_ref[b], near)
    far = tb_ref[N_BUCKETS - 1]
    bias8 = jnp.concatenate([far] * (pps - 1) + [jnp.where(j == nj - 1, near, far)], axis=1)
    maskadd = jnp.where(_in_topk(sc_ref[0], s_idx, thr, cut), 0.0, NEG_INF)
    for p in range(2):
        sl = slice(p * LANES, (p + 1) * LANES)
        kt = jnp.concatenate([_mxu(r[0, sl, :]) for r in k_refs], axis=1)
        vt = jnp.concatenate([_mxu(r[0, sl, :]) for r in v_refs], axis=1)
        x = _dot(qs[p], kt) + head_rows(bias8, p) + maskadd
        update(p, x, lambda pe, vt=vt: _dot_nt(pe, vt))

    @pl.when(j == nj - 1)
    def _():
        valid = (_in_topk(sn_ref[0], past, thr, cut)
                 & (lax.broadcasted_iota(jnp.int32, (1, LANES), 1) == 0))
        madd = jnp.where(valid, 0.0, NEG_INF)
        kn = jnp.broadcast_to(kn_ref[0], (LANES, D_KV))
        vn = jnp.broadcast_to(vn_ref[0], (LANES, D_KV))
        for p in range(2):
            sl = slice(p * LANES, (p + 1) * LANES)
            x = _dot_nt(qs[p], _mxu(kn[:, sl])) + head_rows(tb_ref[0], p) + madd
            update(p, x, lambda pe, sl=sl: _dot(pe, _mxu(vn[:, sl])))
            out = acc_ref[p] / l_ref[p]
            o_ref[0, :, (2 * p) * LANES:(2 * p + 1) * LANES] = jnp.where(lo, out[0:1], out[2:3])
            o_ref[0, :, (2 * p + 1) * LANES:(2 * p + 2) * LANES] = jnp.where(lo, out[1:2], out[3:4])


def _sample_attention(page_table, tab_b, qs, scores, thr, cut, s_new, k_new, v_new,
                      cache_kt, cache_vt, pps):
    db, n_pages = page_table.shape
    page_spec = lambda r: pl.BlockSpec(
        (1, D_KV, PAGE), lambda b, j, pt: (pt[b, j * pps + r], 0, 0))
    per_seq = lambda shape: pl.BlockSpec((1,) + shape, lambda b, j, pt: (b,) + (0,) * len(shape))
    gs = pltpu.PrefetchScalarGridSpec(
        num_scalar_prefetch=1,
        grid=(db, n_pages // pps),
        in_specs=[pl.BlockSpec(tab_b.shape, lambda b, j, pt: (0, 0, 0)),
                  per_seq((2, A_HEADS, LANES)),
                  pl.BlockSpec((1, 1, pps * PAGE), lambda b, j, pt: (b, 0, j)),
                  per_seq((1, LANES)), per_seq((1, LANES)), per_seq((1, LANES)),
                  per_seq((1, D_KV)), per_seq((1, D_KV))]
        + [page_spec(r) for r in range(pps)] * 2,
        out_specs=per_seq((1, D_ATTN)),
        scratch_shapes=[pltpu.VMEM((2, A_HEADS, 1), F32),
                        pltpu.VMEM((2, A_HEADS, 1), F32),
                        pltpu.VMEM((2, A_HEADS, LANES), F32)],
    )
    return pl.pallas_call(
        functools.partial(_sample_attn_kernel, pps),
        grid_spec=gs,
        out_shape=jax.ShapeDtypeStruct((db, 1, D_ATTN), F32),
        compiler_params=pltpu.CompilerParams(
            dimension_semantics=("arbitrary", "arbitrary"), vmem_limit_bytes=VMEM_LIMIT),
        name="sample_attn",
    )(page_table, tab_b, qs, scores, thr, cut, s_new, k_new, v_new,
      *([cache_kt] * pps), *([cache_vt] * pps))


def _permute_heads(a, axis):
    shape = a.shape
    a = a.reshape(shape[:axis] + (A_HEADS, HEAD_DIM) + shape[axis + 1:])
    a = jnp.take(a, jnp.array(HEAD_PERM), axis=axis)
    return a.reshape(shape)


def _pack_w_in(w):
    d = w.shape[0]
    o = [0]
    for wd in (D_ATTN, D_KV, D_KV, D_ATTN, IDX_HEADS * IDX_DIM, IDX_HEADS, IDX_DIM,
               D_H, D_H, D_H, D_H):
        o.append(o[-1] + wd)
    a_q, a_k, a_v, a_g, i_q, i_w, i_k, h_q, h_f, h_i, h_g = (
        w[:, o[n]:o[n + 1]] for n in range(11))
    z = lambda n: jnp.zeros((d, n), w.dtype)
    packed = jnp.concatenate(
        [_permute_heads(a_q, 1), a_k, a_v, _permute_heads(a_g, 1), i_q, i_k, i_k,
         z(IDX_DIM), i_w, z(LANES - IDX_DIM - IDX_HEADS), h_q, h_f, h_i, h_g], axis=1)
    assert packed.shape[1] == C_END
    return packed.astype(MXU_DTYPE)


def kernel(x_prompt, x_sample, cache_k, cache_v, cache_ik, state_hgrn, page_table, rel_bias,
           ln_g, w_in, q_norm_g, k_norm_g, hgrn_lb, hgrn_norm_g, w_out):
    b, s, d = x_prompt.shape
    db, dt, _ = x_sample.shape
    depth, n_pool = cache_k.shape[:2]
    n_pages = page_table.shape[1]
    past = n_pages * PAGE
    assert depth == 1 and dt == 1 and hgrn_lb.shape[0] == 2
    assert s % HG_BLK == 0 and s >= 4 * TOPK and past >= 4 * TOPK
    pps = min(MAX_PAGES_PER_STEP, n_pages)
    assert n_pages % pps == 0 and past % K_CHUNK == 0 and db % 8 == 0
    assert N_BUCKETS // 2 + int(math.log((PAGE + 1) / (N_BUCKETS // 2))
                                / math.log(MAX_DISTANCE / (N_BUCKETS // 2))
                                * (N_BUCKETS - N_BUCKETS // 2)) >= N_BUCKETS - 1

    w_pack = _pack_w_in(w_in[0])
    w_out_pack = jnp.concatenate(
        [_permute_heads(w_out[0][:D_ATTN], 0), w_out[0][D_ATTN:]], axis=0).astype(MXU_DTYPE)
    qg = jnp.tile(q_norm_g[0], A_HEADS)[None]
    kg = jnp.tile(k_norm_g[0], A_KV_HEADS)[None]
    hng = jnp.tile(hgrn_norm_g[0], H_HEADS)[None]
    bd = _block_diag(D_ATTN, HEAD_DIM, 1.0 / HEAD_DIM, MXU_DTYPE)
    lng = ln_g[0][None]

    pp = _project(x_prompt.reshape(b * s, d), lng, w_pack, qg, kg, hgrn_lb, bd, 256)
    r3 = lambda a: a.reshape(b, s, a.shape[-1])
    lp = -(-(s + Q_BLK) // K_CHUNK) * K_CHUNK
    padk = lambda a: jnp.pad(r3(a), ((0, 0), (Q_BLK, lp - s - Q_BLK), (0, 0)))
    tr = lambda a: a.transpose(0, 2, 1)
    iw_p = r3(pp["ikw"])[:, :, IDX_DIM:IDX_DIM + IDX_HEADS]
    a_out = _attn_prompt(rel_bias, _bias_tiles(rel_bias), tr(r3(pp["q"])), tr(r3(pp["iq"])),
                         tr(iw_p), padk(pp["kb"]), tr(padk(pp["vb"])), padk(pp["ik2"]))
    h_out, st = _hgrn_prompt(r3(pp["hq"]), r3(pp["hg"]), r3(pp["hk"]), r3(pp["hv"]))
    y_prompt = _merge(x_prompt.reshape(b * s, d), a_out.reshape(b * s, D_ATTN), pp["sga"],
                      h_out.reshape(b * s, D_H), pp["sgh"], hng, bd, w_out_pack, 256)
    st = st.reshape(b, D_H // LANES, 2, H_KEY, 2, H_KEY)
    s_prompt = jnp.stack([st[:, :, e, :, e, :] for e in range(2)], axis=2)
    s_prompt = s_prompt.reshape(b, H_HEADS, H_KEY, H_KEY).transpose(0, 1, 3, 2)

    sp = _project(x_sample.reshape(db, d), lng, w_pack, qg, kg, hgrn_lb, bd, db)
    ik_s = sp["ikw"][:, :IDX_DIM]
    iw_s = sp["ikw"][:, IDX_DIM:IDX_DIM + IDX_HEADS]
    scores, s_new = _sample_scores(
        page_table, sp["iq"].reshape(db, IDX_HEADS, IDX_DIM), iw_s.reshape(db, IDX_HEADS, 1),
        ik_s.reshape(db, 1, IDX_DIM), cache_ik[0].transpose(0, 2, 1), pps)
    thr, cut = _sample_threshold(scores.reshape(db, past).T, s_new[:, :, 0].T)
    lane_b = lambda a: jnp.broadcast_to(a.reshape(db, 1, 1), (db, 1, LANES))
    qg4 = sp["q"].reshape(db, 2, 2, 2, HEAD_DIM)
    rows = []
    for p in range(2):
        g0lo, g0hi = qg4[:, p, 0, 0], qg4[:, p, 0, 1]
        g1lo, g1hi = qg4[:, p, 1, 0], qg4[:, p, 1, 1]
        z1 = jnp.zeros_like(g0lo)
        pr = jnp.stack([jnp.concatenate([g0lo, z1], -1), jnp.concatenate([g1lo, z1], -1),
                        jnp.concatenate([z1, g0hi], -1), jnp.concatenate([z1, g1hi], -1)]
                       + [jnp.zeros((db, LANES), sp["q"].dtype)] * 4, axis=1)
        rows.append(pr)
    qs = jnp.stack(rows, axis=1)
    tab_b = jnp.broadcast_to(rel_bias[:, :, None] * LOG2E, (N_BUCKETS, A_HEADS, LANES))
    page_t = lambda c: c[0].transpose(0, 2, 3, 1).reshape(n_pool, D_KV, PAGE)
    a_out_s = _sample_attention(
        page_table, tab_b, qs, scores, lane_b(thr), lane_b(cut), s_new,
        sp["kf"].reshape(db, 1, D_KV), sp["vf"].reshape(db, 1, D_KV),
        page_t(cache_k), page_t(cache_v), pps)
    h_out_s, s_sample = _hgrn_sample(sp["hq"], sp["hg"], sp["hk"], sp["hv"],
                                     state_hgrn[0].transpose(1, 2, 3, 0))
    s_sample = s_sample.transpose(3, 0, 1, 2)
    y_sample = _merge(x_sample.reshape(db, d), a_out_s.reshape(db, D_ATTN), sp["sga"],
                      h_out_s.reshape(db, D_H), sp["sgh"], hng, bd, w_out_pack, db)

    kv5 = lambda a, n: a.reshape(1, n, -1, A_KV_HEADS, HEAD_DIM)
    return (y_prompt.reshape(b, s, d), y_sample.reshape(db, 1, d),
            kv5(pp["kf"], b), kv5(pp["vf"], b),
            pp["ikw"][:, :IDX_DIM].reshape(1, b, s, IDX_DIM), s_prompt[None],
            kv5(sp["kf"], db), kv5(sp["vf"], db),
            ik_s.reshape(1, db, 1, IDX_DIM), s_sample[None])
```

```python
import functools
import math

import jax
import jax.numpy as jnp
from jax import lax
from jax.experimental import pallas as pl
from jax.experimental.pallas import tpu as pltpu

F32 = jnp.float32
BF16 = jnp.bfloat16
MXU_DTYPE = BF16

HEAD_DIM = 64
A_HEADS = 8
A_KV_HEADS = 4
D_ATTN = A_HEADS * HEAD_DIM
D_KV = A_KV_HEADS * HEAD_DIM
IDX_HEADS = 8
IDX_DIM = 64
H_HEADS = 8
H_KEY = 64
D_H = H_HEADS * H_KEY
TOPK = 256
PAGE = 128
N_BUCKETS = 32
MAX_DISTANCE = 128
EPS = 1e-6
LANES = 128
Q_BLK = 128
K_CHUNK = 512
FAR_CHUNK = 1024
HG_BLK = 256
HG_CHUNK = 16
V_ROWS = LANES + 16
MAX_PAGES_PER_STEP = 32
VMEM_LIMIT = 48 * 1024 * 1024
NEG_INF = float("-inf")
LOG2E = math.log2(math.e)
KEY_NEG_INF = -2139095041

C_AQ, C_AK, C_AV, C_AG, C_IQ, C_IK, C_IW, C_HQ, C_HF, C_HI, C_HG, C_END = (
    0, 512, 768, 1024, 1536, 2048, 2176, 2304, 2816, 3328, 3840, 4352)
HEAD_PERM = (0, 2, 1, 3, 4, 6, 5, 7)


def _mxu(x):
    return x.astype(MXU_DTYPE)


def _dot(a, b):
    return jnp.dot(a, b, preferred_element_type=F32)


def _dot_nt(a, b):
    return lax.dot_general(a, b, (((1,), (1,)), ((), ())), preferred_element_type=F32)


def _dot_tn(a, b):
    return lax.dot_general(a, b, (((0,), (0,)), ((), ())), preferred_element_type=F32)


def _split_dot(x, w, parts):
    acc = None
    r = x
    for _ in range(parts):
        p = _mxu(r)
        t = _dot(p, w)
        acc = t if acc is None else acc + t
        r = r - p.astype(F32)
    return acc


def _sigmoid(x):
    return 1.0 / (1.0 + jnp.exp(-x))


def _silu(x):
    return x * _sigmoid(x)


def _t5_bucket(n):
    n = jnp.maximum(n, 0)
    max_exact = N_BUCKETS // 2
    nf = jnp.maximum(n, 1).astype(F32)
    large = max_exact + jnp.floor(jnp.log(nf / max_exact) / math.log(MAX_DISTANCE / max_exact)
                                  * (N_BUCKETS - max_exact)).astype(jnp.int32)
    large = jnp.minimum(large, N_BUCKETS - 1)
    return jnp.where(n < max_exact, n, large)


def _block_diag(n, blk, val, dtype):
    r = lax.broadcasted_iota(jnp.int32, (n, n), 0) // blk
    c = lax.broadcasted_iota(jnp.int32, (n, n), 1) // blk
    return jnp.where(r == c, val, 0.0).astype(dtype)


def _proj_kernel(x_ref, lng_ref, w_ref, qg_ref, kg_ref, lb_ref, bd_ref,
                 q_ref, iq_ref, kf_ref, kb_ref, vf_ref, vb_ref, ikw_ref, ik2_ref,
                 sga_ref, hq_ref, hg_ref, hk_ref, hv_ref, sgh_ref):
    x = x_ref[...]
    ms = jnp.mean(x * x, axis=-1, keepdims=True)
    xb = _mxu(x * lax.rsqrt(ms + EPS) * lng_ref[...])

    def seg(a, b):
        return _dot(xb, w_ref[:, a:b])

    bd = bd_ref[...]

    aq = seg(C_AQ, C_AK)
    msq = _split_dot(aq * aq, bd, 2)
    q_ref[...] = (aq * lax.rsqrt(msq + EPS) * qg_ref[...]
                  * (HEAD_DIM ** -0.5 * LOG2E)).astype(q_ref.dtype)

    ak = seg(C_AK, C_AV)
    msk = _split_dot(ak * ak, bd[:D_KV, :D_KV], 2)
    k = ak * lax.rsqrt(msk + EPS) * kg_ref[...]
    kf_ref[...] = k
    kb_ref[...] = k.astype(kb_ref.dtype)

    v = seg(C_AV, C_AG)
    vf_ref[...] = v
    vb_ref[...] = v.astype(vb_ref.dtype)

    sga_ref[...] = _silu(seg(C_AG, C_IQ))
    iq_ref[...] = seg(C_IQ, C_IK).astype(iq_ref.dtype)

    ikk = seg(C_IK, C_IW)
    iww = seg(C_IW, C_HQ) * (IDX_HEADS ** -0.5 * IDX_DIM ** -0.5)
    ik2_ref[...] = ikk.astype(ik2_ref.dtype)
    lane = lax.broadcasted_iota(jnp.int32, ikk.shape, 1)
    ikw_ref[...] = jnp.where(lane < IDX_DIM, ikk, iww)

    def put_pairs(ref, val):
        for p in range(D_H // LANES):
            ref[p] = val[:, p * LANES:(p + 1) * LANES]

    put_pairs(hq_ref, _silu(seg(C_HQ, C_HF)) * H_KEY ** -0.5)
    lbp = lb_ref[...]
    mx = jnp.max(lbp, axis=0, keepdims=True)
    e = jnp.exp(lbp - mx)
    lb = e[0:1] / jnp.sum(e, axis=0, keepdims=True)
    f = lb + (1.0 - lb) * _sigmoid(seg(C_HF, C_HI))
    put_pairs(hg_ref, jnp.log(f))
    put_pairs(hk_ref, 1.0 - f)
    put_pairs(hv_ref, seg(C_HI, C_HG))
    sgh_ref[...] = _silu(seg(C_HG, C_END))


def _project(x2d, ln_g, w_pack, qg, kg, lbp, bd, tm):
    n, d = x2d.shape
    row = lambda w: pl.BlockSpec((tm, w), lambda i: (i, 0))
    full = lambda a: pl.BlockSpec(a.shape, lambda i: (0,) * a.ndim)
    outs = [("q", D_ATTN, MXU_DTYPE), ("iq", D_ATTN, MXU_DTYPE), ("kf", D_KV, F32),
            ("kb", D_KV, MXU_DTYPE), ("vf", D_KV, F32), ("vb", D_KV, MXU_DTYPE),
            ("ikw", LANES, F32), ("ik2", LANES, MXU_DTYPE), ("sga", D_ATTN, F32),
            ("hq", 0, F32), ("hg", 0, F32), ("hk", 0, F32), ("hv", 0, F32),
            ("sgh", D_H, F32)]
    n_pairs = D_H // LANES
    pair_spec = pl.BlockSpec((n_pairs, tm, LANES), lambda i: (0, i, 0))
    res = pl.pallas_call(
        _proj_kernel,
        grid=(n // tm,),
        in_specs=[row(d), full(ln_g), full(w_pack), full(qg), full(kg), full(lbp), full(bd)],
        out_specs=[row(w) if w else pair_spec for _, w, _ in outs],
        out_shape=[jax.ShapeDtypeStruct((n, w) if w else (n_pairs, n, LANES), dt)
                   for _, w, dt in outs],
        compiler_params=pltpu.CompilerParams(
            dimension_semantics=("arbitrary",), vmem_limit_bytes=VMEM_LIMIT),
        name="proj",
    )(x2d, ln_g, w_pack, qg, kg, lbp, bd)
    return dict(zip([o[0] for o in outs], res))


def _bias_tile_kernel(tab_ref, o_ref):
    j = lax.broadcasted_iota(jnp.int32, (2 * Q_BLK, Q_BLK), 0)
    r = lax.broadcasted_iota(jnp.int32, (2 * Q_BLK, Q_BLK), 1)
    bucket = _t5_bucket(Q_BLK + r - j)
    for h in range(A_HEADS):
        acc = jnp.zeros((2 * Q_BLK, Q_BLK), F32)
        for b in range(N_BUCKETS):
            acc = jnp.where(bucket == b, tab_ref[b, h] * LOG2E, acc)
        o_ref[h] = acc


def _bias_tiles(rel_bias):
    return pl.pallas_call(
        _bias_tile_kernel,
        in_specs=[pl.BlockSpec(memory_space=pltpu.SMEM)],
        out_shape=jax.ShapeDtypeStruct((A_HEADS, 2 * Q_BLK, Q_BLK), F32),
        name="bias_tiles",
    )(rel_bias)


def _key_to_f32(key):
    bits = key ^ ((key >> 31) & jnp.int32(0x7FFFFFFF))
    return lax.bitcast_convert_type(bits, F32)


def _topk_cut(sc_ref, nch, cut_ref, extra=None):
    ncol = sc_ref.shape[1]
    nrows = nch * K_CHUNK
    nbits = int(sc_ref.shape[0]).bit_length()
    groups = K_CHUNK // 8

    def count(pred):
        def body(c, acc):
            r0 = pl.multiple_of(c * K_CHUNK, K_CHUNK)
            hit = jnp.where(pred(sc_ref[pl.ds(r0, K_CHUNK), :], r0), 1, 0).astype(jnp.int32)
            return acc + jnp.sum(hit.reshape(groups, 8, ncol), axis=0, dtype=jnp.int32)

        acc = lax.fori_loop(0, nch, body, jnp.zeros((8, ncol), jnp.int32))
        cnt = jnp.sum(acc, axis=0, keepdims=True, dtype=jnp.int32)
        if extra is not None:
            cnt = cnt + jnp.where(pred(extra, nrows), 1, 0).astype(jnp.int32)
        return cnt

    def bit_body(it, carry):
        key, n_ge = carry
        cand = key + jnp.left_shift(jnp.int32(1), 31 - it)
        cf = _key_to_f32(cand)
        cnt = count(lambda v, r0: v >= cf)
        ok = cnt >= TOPK
        return jnp.where(ok, cand, key), jnp.where(ok, cnt, n_ge)

    key0 = jnp.full((1, ncol), jnp.iinfo(jnp.int32).min, jnp.int32)
    key, n_ge = lax.fori_loop(0, 32, bit_body, (key0, jnp.full((1, ncol), TOPK, jnp.int32)))
    below = key < KEY_NEG_INF
    thr = _key_to_f32(jnp.maximum(key, KEY_NEG_INF))
    cut_ref[...] = jnp.full((1, ncol), jnp.iinfo(jnp.int32).max, jnp.int32)

    @pl.when(jnp.max(jnp.where(below, TOPK + 1, n_ge)) > TOPK)
    def _():
        need = TOPK - count(lambda v, r0: v > thr)

        def row_of(v, r0):
            return r0 + lax.broadcasted_iota(jnp.int32, v.shape, 0)

        def idx_body(it, x):
            cand = x + jnp.left_shift(jnp.int32(1), nbits - 1 - it)
            g = count(lambda v, r0: (v == thr) & (row_of(v, r0) < cand))
            return jnp.where(g < need, cand, x)

        cut_ref[...] = lax.fori_loop(0, nbits, idx_body, jnp.zeros((1, ncol), jnp.int32))

    return thr


def _in_topk(sc, col, thr, cut):
    return (sc > thr) | ((sc == thr) & (col <= cut))


def _row_halves(x):
    half = x.shape[0] // 2
    zero = jnp.zeros((half, x.shape[1]), x.dtype)
    return (jnp.concatenate([x[:half], zero], axis=0), jnp.concatenate([zero, x[half:]], axis=0))


def _attn_prompt_kernel(tab_ref, qt_ref, iqt_ref, iwt_ref, kb_ref, vt_ref, ik2_ref, bt_ref,
                        o_ref, sc_ref, cut_ref, m_ref, acc_ref):
    i = pl.program_id(1)
    t_pos = i * Q_BLK + lax.broadcasted_iota(jnp.int32, (1, Q_BLK), 1)

    @pl.when((pl.program_id(0) == 0) & (i == 0))
    def _():
        sc_ref[...] = jnp.full(sc_ref.shape, NEG_INF, F32)

    iqt = iqt_ref[0]
    cols = []
    for g in range(D_ATTN // LANES):
        cols.extend(_row_halves(iqt[g * LANES:(g + 1) * LANES]))
    iq_stack = jnp.concatenate(cols, axis=1)
    iw = iwt_ref[0]
    nch = (i + 5) // 4

    def score_body(c, carry):
        r0 = pl.multiple_of(c * K_CHUNK, K_CHUNK)
        s_all = _dot(ik2_ref[0, pl.ds(r0, K_CHUNK), :], iq_stack)
        sc = jnp.zeros((K_CHUNK, Q_BLK), F32)
        for h in range(IDX_HEADS):
            sc = sc + jnp.maximum(s_all[:, h * Q_BLK:(h + 1) * Q_BLK], 0.0) * iw[h:h + 1]
        s_glob = r0 - Q_BLK + lax.broadcasted_iota(jnp.int32, (K_CHUNK, Q_BLK), 0)
        valid = (s_glob >= 0) & (s_glob <= t_pos)
        sc_ref[pl.ds(r0, K_CHUNK), :] = jnp.where(valid, sc, NEG_INF)
        return carry

    lax.fori_loop(0, nch, score_body, 0)
    thr = _topk_cut(sc_ref, nch, cut_ref)
    cut = cut_ref[...]

    m_ref[...] = jnp.full(m_ref.shape, NEG_INF, F32)
    acc_ref[...] = jnp.zeros(acc_ref.shape, F32)

    qt = qt_ref[0]
    q_pairs = []
    for p in range(2):
        g0 = _row_halves(qt[(2 * p) * LANES:(2 * p + 1) * LANES])
        g1 = _row_halves(qt[(2 * p + 1) * LANES:(2 * p + 2) * LANES])
        q_pairs.append(jnp.concatenate([g0[0], g1[0], g0[1], g1[1]], axis=1))

    def attend(r0, width, valid, bias_fn, const_fn):
        madd = jnp.where(valid, 0.0, NEG_INF)
        for p in range(2):
            kc = kb_ref[0, pl.ds(r0, width), p * LANES:(p + 1) * LANES]
            vc = vt_ref[0, p * V_ROWS:(p + 1) * V_ROWS, pl.ds(r0, width)]
            st = _dot(kc, q_pairs[p])
            ps, alphas = [], []
            for a in range(4):
                h = 4 * p + a
                x = st[:, a * Q_BLK:(a + 1) * Q_BLK] + madd
                bias = bias_fn(h)
                if bias is not None:
                    x = x + bias
                c = const_fn(h)
                m_old = m_ref[h:h + 1]
                m_new = jnp.maximum(m_old, jnp.max(x, axis=0, keepdims=True) + c)
                m_safe = jnp.where(m_new == NEG_INF, 0.0, m_new)
                m_ref[h:h + 1] = m_new
                ps.append(_mxu(jnp.exp2(x - (m_safe - c))))
                alphas.append(jnp.exp2(m_old - m_safe))
            pv = _dot(vc, jnp.concatenate(ps, axis=1))
            acc_ref[p] = acc_ref[p] * jnp.concatenate(alphas, axis=1) + pv

    def far_body(c, carry):
        r0 = pl.multiple_of(c * FAR_CHUNK, FAR_CHUNK)
        row = r0 + lax.broadcasted_iota(jnp.int32, (FAR_CHUNK, Q_BLK), 0)
        valid = ((row >= Q_BLK) & (row < i * Q_BLK)
                 & _in_topk(sc_ref[pl.ds(r0, FAR_CHUNK), :], row, thr, cut))
        attend(r0, FAR_CHUNK, valid, lambda h: None, lambda h: tab_ref[N_BUCKETS - 1, h] * LOG2E)
        return carry

    lax.fori_loop(0, (i * Q_BLK + FAR_CHUNK - 1) // FAR_CHUNK, far_body, 0)

    r0 = pl.multiple_of(i * Q_BLK, Q_BLK)
    row = r0 + lax.broadcasted_iota(jnp.int32, (2 * Q_BLK, Q_BLK), 0)
    valid = ((row >= Q_BLK) & (row - Q_BLK <= t_pos)
             & _in_topk(sc_ref[pl.ds(r0, 2 * Q_BLK), :], row, thr, cut))
    attend(r0, 2 * Q_BLK, valid, lambda h: bt_ref[h], lambda h: 0.0)

    top = lax.broadcasted_iota(jnp.int32, (LANES, Q_BLK), 0) < HEAD_DIM
    for p in range(2):
        acc = acc_ref[p]
        outs = [acc[:LANES, a * Q_BLK:(a + 1) * Q_BLK] / acc[LANES:LANES + 1, a * Q_BLK:(a + 1) * Q_BLK]
                for a in range(4)]
        o_ref[0, :, (2 * p) * LANES:(2 * p + 1) * LANES] = jnp.where(top, outs[0], outs[2]).T
        o_ref[0, :, (2 * p + 1) * LANES:(2 * p + 2) * LANES] = jnp.where(top, outs[1], outs[3]).T


def _attn_prompt(rel_bias, bias_tiles, qt, iqt, iwt, kbp, vtp, ik2p):
    b, _, s = qt.shape
    lp = kbp.shape[1]
    blk_t = lambda r: pl.BlockSpec((1, r, Q_BLK), lambda bi, i: (bi, 0, i))
    return pl.pallas_call(
        _attn_prompt_kernel,
        grid=(b, s // Q_BLK),
        in_specs=[pl.BlockSpec(memory_space=pltpu.SMEM),
                  blk_t(D_ATTN), blk_t(D_ATTN), blk_t(IDX_HEADS),
                  pl.BlockSpec((1, lp, D_KV), lambda bi, i: (bi, 0, 0)),
                  pl.BlockSpec((1, 2 * V_ROWS, lp), lambda bi, i: (bi, 0, 0)),
                  pl.BlockSpec((1, lp, LANES), lambda bi, i: (bi, 0, 0)),
                  pl.BlockSpec(bias_tiles.shape, lambda bi, i: (0, 0, 0))],
        out_specs=pl.BlockSpec((1, Q_BLK, D_ATTN), lambda bi, i: (bi, i, 0)),
        out_shape=jax.ShapeDtypeStruct((b, s, D_ATTN), F32),
        scratch_shapes=[pltpu.VMEM((lp, Q_BLK), F32),
                        pltpu.VMEM((1, Q_BLK), jnp.int32),
                        pltpu.VMEM((A_HEADS, Q_BLK), F32),
                        pltpu.VMEM((2, V_ROWS, 4 * Q_BLK), F32)],
        compiler_params=pltpu.CompilerParams(
            dimension_semantics=("arbitrary", "arbitrary"), vmem_limit_bytes=VMEM_LIMIT),
        name="attn_prompt",
    )(rel_bias, qt, iqt, iwt, kbp, vtp, ik2p, bias_tiles)


def _hgrn_prompt_kernel(q_ref, g_ref, k_ref, v_ref, o_ref, st_ref, b_ref, oi_ref,
                        tq_ref, tk_ref, tv_ref, tb_ref):
    j = pl.program_id(1)
    n_pairs = D_H // LANES
    n_chunks = HG_BLK // HG_CHUNK

    @pl.when(j == 0)
    def _():
        st_ref[...] = jnp.zeros(st_ref.shape, F32)

    def lanes(ref, rows):
        return jnp.concatenate([ref[p, rows, :] for p in range(n_pairs)], axis=1)

    r = lax.broadcasted_iota(jnp.int32, (HG_BLK, HG_BLK), 0)
    c = lax.broadcasted_iota(jnp.int32, (HG_BLK, HG_BLK), 1)
    tri = jnp.where(c <= r, 1.0, 0.0).astype(MXU_DTYPE)
    bcum = _split_dot_left(tri, lanes(g_ref, slice(None)), 3)
    for p in range(n_pairs):
        b_ref[p] = bcum[:, p * LANES:(p + 1) * LANES]

    for t in range(HG_CHUNK):
        rows = pl.ds(t, n_chunks, stride=HG_CHUNK)
        tq_ref[t] = lanes(q_ref, rows)
        tk_ref[t] = lanes(k_ref, rows)
        tv_ref[t] = lanes(v_ref, rows)
        tb_ref[t] = lanes(b_ref, rows)

    half = D_H // 2
    bd_half = _block_diag(half, H_KEY, 1.0, MXU_DTYPE)
    for t in range(HG_CHUNK):
        qt = tq_ref[t]
        bt = tb_ref[t]
        parts = [qt * tk_ref[s] * jnp.exp(bt - tb_ref[s]) for s in range(t)]
        parts.append(qt * tk_ref[t])
        pr = _mxu(jnp.concatenate(parts, axis=0))
        rs = jnp.concatenate([_dot(pr[:, :half], bd_half), _dot(pr[:, half:], bd_half)], axis=1)
        acc = rs[0:n_chunks] * tv_ref[0]
        for s in range(1, t + 1):
            acc = acc + rs[s * n_chunks:(s + 1) * n_chunks] * tv_ref[s]
        for p in range(n_pairs):
            oi_ref[p, pl.ds(t, n_chunks, stride=HG_CHUNK), :] = acc[:, p * LANES:(p + 1) * LANES]

    lane_r = lax.broadcasted_iota(jnp.int32, (LANES, LANES), 0) // H_KEY
    lane_c = lax.broadcasted_iota(jnp.int32, (LANES, LANES), 1) // H_KEY
    same_head = lane_r == lane_c
    for cch in range(n_chunks):
        rows = slice(cch * HG_CHUNK, (cch + 1) * HG_CHUNK)
        last = slice((cch + 1) * HG_CHUNK - 1, (cch + 1) * HG_CHUNK)
        prev = slice(cch * HG_CHUNK - 1, cch * HG_CHUNK)
        outs = []
        for p in range(n_pairs):
            bch = b_ref[p, rows, :]
            b0 = b_ref[p, prev, :] if cch > 0 else jnp.zeros((1, LANES), F32)
            bl = b_ref[p, last, :]
            qi = _mxu(q_ref[p, rows, :] * jnp.exp(bch - b0))
            kl = _mxu(k_ref[p, rows, :] * jnp.exp(bl - bch))
            st = st_ref[0, p]
            outs.append(oi_ref[p, rows, :] + _dot_nt(qi, _mxu(st)))
            ds = _dot_tn(_mxu(v_ref[p, rows, :]), kl)
            st_ref[0, p] = st * jnp.exp(bl - b0) + jnp.where(same_head, ds, 0.0)
        o_ref[0, rows, :] = jnp.concatenate(outs, axis=1)


def _split_dot_left(w, x, parts):
    acc = None
    r = x
    for _ in range(parts):
        p = _mxu(r)
        t = _dot(w, p)
        acc = t if acc is None else acc + t
        r = r - p.astype(F32)
    return acc


def _hgrn_prompt(hq, hg, hk, hv, b, s):
    n_pairs = D_H // LANES
    nblk = s // HG_BLK
    blk = pl.BlockSpec((n_pairs, HG_BLK, LANES), lambda bi, j: (0, bi * nblk + j, 0))
    return pl.pallas_call(
        _hgrn_prompt_kernel,
        grid=(b, nblk),
        in_specs=[blk, blk, blk, blk],
        out_specs=[pl.BlockSpec((1, HG_BLK, D_H), lambda bi, j: (bi, j, 0)),
                   pl.BlockSpec((1, n_pairs, LANES, LANES), lambda bi, j: (bi, 0, 0, 0))],
        out_shape=[jax.ShapeDtypeStruct((b, s, D_H), F32),
                   jax.ShapeDtypeStruct((b, n_pairs, LANES, LANES), F32)],
        scratch_shapes=[pltpu.VMEM((n_pairs, HG_BLK, LANES), F32)] * 2
        + [pltpu.VMEM((HG_CHUNK, HG_BLK // HG_CHUNK, D_H), F32)] * 4,
        compiler_params=pltpu.CompilerParams(
            dimension_semantics=("arbitrary", "arbitrary"), vmem_limit_bytes=VMEM_LIMIT),
        name="hgrn_prompt",
    )(hq, hg, hk, hv)


def _hgrn_sample_kernel(q_ref, g_ref, k_ref, v_ref, s_ref, o_ref, sn_ref):
    q = q_ref[...]
    e = jnp.exp(g_ref[...])
    k = k_ref[...]
    v = v_ref[...]
    o = jnp.zeros(v.shape, F32)
    for kk in range(H_KEY):
        sn = e[kk:kk + 1] * s_ref[0, kk] + k[kk:kk + 1] * v
        sn_ref[0, kk] = sn
        o = o + q[kk:kk + 1] * sn
    o_ref[...] = o


def _hgrn_sample(hq, hg, hk, hv, state_t):
    db = hq.shape[0]
    hspec = pl.BlockSpec((H_KEY, db), lambda h: (h, 0))
    sspec = pl.BlockSpec((1, H_KEY, H_KEY, db), lambda h: (h, 0, 0, 0))
    o_t, s_new = pl.pallas_call(
        _hgrn_sample_kernel,
        grid=(H_HEADS,),
        in_specs=[hspec, hspec, hspec, hspec, sspec],
        out_specs=[hspec, sspec],
        out_shape=[jax.ShapeDtypeStruct((D_H, db), F32),
                   jax.ShapeDtypeStruct(state_t.shape, F32)],
        compiler_params=pltpu.CompilerParams(
            dimension_semantics=("arbitrary",), vmem_limit_bytes=VMEM_LIMIT),
        name="hgrn_sample",
    )(hq.T, hg.T, hk.T, hv.T, state_t)
    return o_t.T, s_new


def _merge_kernel(x_ref, a_ref, sga_ref, h_ref, sgh_ref, hng_ref, bd_ref, w_ref, y_ref):
    h = h_ref[...]
    msq = _split_dot(h * h, bd_ref[...], 2)
    hn = h * lax.rsqrt(msq + EPS) * hng_ref[...]
    ma = _mxu(a_ref[...] * sga_ref[...])
    mh = _mxu(hn * sgh_ref[...])
    y_ref[...] = x_ref[...] + _dot(ma, w_ref[0:D_ATTN, :]) + _dot(mh, w_ref[D_ATTN:, :])


def _merge(x2d, a_out, sga, h_out, sgh, hng, bd, w_pack, tm):
    n, d = x2d.shape
    row = lambda w: pl.BlockSpec((tm, w), lambda i: (i, 0))
    full = lambda a: pl.BlockSpec(a.shape, lambda i: (0,) * a.ndim)
    return pl.pallas_call(
        _merge_kernel,
        grid=(n // tm,),
        in_specs=[row(d), row(D_ATTN), row(D_ATTN), row(D_H), row(D_H),
                  full(hng), full(bd), full(w_pack)],
        out_specs=row(d),
        out_shape=jax.ShapeDtypeStruct((n, d), F32),
        compiler_params=pltpu.CompilerParams(
            dimension_semantics=("arbitrary",), vmem_limit_bytes=VMEM_LIMIT),
        name="merge",
    )(x2d, a_out, sga, h_out, sgh, hng, bd, w_pack)


def _sample_score_kernel(pps, pt_ref, iqh_ref, iwc_ref, ikn_ref, *refs):
    ik_refs = refs[:pps]
    sc_ref, sn_ref = refs[pps:]
    iqh = iqh_ref[0]
    iwc = iwc_ref[0]
    ikt = jnp.concatenate([_mxu(r[0]) for r in ik_refs], axis=1)
    s = _dot(iqh, ikt)
    sc_ref[0] = jnp.sum(jnp.maximum(s, 0.0) * iwc, axis=0, keepdims=True)
    prod = iqh.astype(F32) * _mxu(ikn_ref[0]).astype(F32)
    s_new = jnp.sum(prod, axis=1, keepdims=True)
    s_new = jnp.sum(jnp.maximum(s_new, 0.0) * iwc, axis=0, keepdims=True)
    sn_ref[0] = jnp.broadcast_to(s_new, (1, LANES))


def _sample_scores(page_table, iqh, iwc, ikn, cache_ikt, pps):
    db, n_pages = page_table.shape
    page_spec = lambda r: pl.BlockSpec(
        (1, IDX_DIM, PAGE), lambda b, j, pt: (pt[b, j * pps + r], 0, 0))
    gs = pltpu.PrefetchScalarGridSpec(
        num_scalar_prefetch=1,
        grid=(db, n_pages // pps),
        in_specs=[pl.BlockSpec((1, IDX_HEADS, IDX_DIM), lambda b, j, pt: (b, 0, 0)),
                  pl.BlockSpec((1, IDX_HEADS, 1), lambda b, j, pt: (b, 0, 0)),
                  pl.BlockSpec((1, 1, IDX_DIM), lambda b, j, pt: (b, 0, 0))]
        + [page_spec(r) for r in range(pps)],
        out_specs=[pl.BlockSpec((1, 1, pps * PAGE), lambda b, j, pt: (b, 0, j)),
                   pl.BlockSpec((1, 1, LANES), lambda b, j, pt: (b, 0, 0))],
    )
    return pl.pallas_call(
        functools.partial(_sample_score_kernel, pps),
        grid_spec=gs,
        out_shape=[jax.ShapeDtypeStruct((db, 1, n_pages * PAGE), F32),
                   jax.ShapeDtypeStruct((db, 1, LANES), F32)],
        compiler_params=pltpu.CompilerParams(
            dimension_semantics=("arbitrary", "arbitrary"), vmem_limit_bytes=VMEM_LIMIT),
        name="sample_scores",
    )(page_table, iqh, iwc, ikn, *([cache_ikt] * pps))


def _sample_thr_kernel(sc_ref, sn_ref, thr_ref, cut_ref):
    past = sc_ref.shape[0]
    thr_ref[...] = _topk_cut(sc_ref, past // K_CHUNK, cut_ref, extra=sn_ref[...])


def _sample_threshold(scores_t, s_new_t):
    db = scores_t.shape[1]
    return pl.pallas_call(
        _sample_thr_kernel,
        out_shape=[jax.ShapeDtypeStruct((1, db), F32),
                   jax.ShapeDtypeStruct((1, db), jnp.int32)],
        compiler_params=pltpu.CompilerParams(vmem_limit_bytes=VMEM_LIMIT),
        name="sample_threshold",
    )(scores_t, s_new_t)


def _sample_attn_kernel(pps, pt_ref, tb_ref, qs_ref, sc_ref, thr_ref, cut_ref, sn_ref,
                        kn_ref, vn_ref, *refs):
    k_refs = refs[:pps]
    v_refs = refs[pps:2 * pps]
    o_ref, m_ref, l_ref, acc_ref = refs[2 * pps:]
    j = pl.program_id(1)
    nj = pl.num_programs(1)
    n = pps * PAGE
    past = nj * n

    @pl.when(j == 0)
    def _():
        m_ref[...] = jnp.full(m_ref.shape, NEG_INF, F32)
        l_ref[...] = jnp.zeros(l_ref.shape, F32)
        acc_ref[...] = jnp.zeros(acc_ref.shape, F32)

    thr = thr_ref[0, :, 0:1]
    cut = cut_ref[0, :, 0:1]
    qs = qs_ref[0]
    lo = lax.broadcasted_iota(jnp.int32, (1, LANES), 1) < HEAD_DIM

    def update(p, x, pv_fn):
        m_old = m_ref[p]
        m_new = jnp.maximum(m_old, jnp.max(x, axis=1, keepdims=True))
        m_safe = jnp.where(m_new == NEG_INF, 0.0, m_new)
        alpha = jnp.exp2(m_old - m_safe)
        pe = jnp.exp2(x - m_safe)
        l_ref[p] = alpha * l_ref[p] + jnp.sum(pe, axis=1, keepdims=True)
        m_ref[p] = m_new
        acc_ref[p] = alpha * acc_ref[p] + pv_fn(_mxu(pe))

    def head_rows(b8, p):
        return jnp.concatenate([b8[4 * p:4 * p + 4], jnp.zeros((4, b8.shape[1]), F32)], axis=0)

    s_idx = j * n + lax.broadcasted_iota(jnp.int32, (1, n), 1)
    s_last = s_idx[:, n - PAGE:]
    bucket = _t5_bucket(past - s_last)
    near = jnp.zeros((A_HEADS, PAGE), F32)
    for b in range(N_BUCKETS):
        near = jnp.where(bucket == b, tb_ref[b], near)
    far = tb_ref[N_BUCKETS - 1]
    bias8 = jnp.concatenate([far] * (pps - 1) + [jnp.where(j == nj - 1, near, far)], axis=1)
    maskadd = jnp.where(_in_topk(sc_ref[0], s_idx, thr, cut), 0.0, NEG_INF)
    for p in range(2):
        sl = slice(p * LANES, (p + 1) * LANES)
        kt = jnp.concatenate([_mxu(r[0, sl, :]) for r in k_refs], axis=1)
        vt = jnp.concatenate([_mxu(r[0, sl, :]) for r in v_refs], axis=1)
        x = _dot(qs[p], kt) + head_rows(bias8, p) + maskadd
        update(p, x, lambda pe, vt=vt: _dot_nt(pe, vt))

    @pl.when(j == nj - 1)
    def _():
        valid = (_in_topk(sn_ref[0], past, thr, cut)
                 & (lax.broadcasted_iota(jnp.int32, (1, LANES), 1) == 0))
        madd = jnp.where(valid, 0.0, NEG_INF)
        kn = jnp.broadcast_to(kn_ref[0], (LANES, D_KV))
        vn = jnp.broadcast_to(vn_ref[0], (LANES, D_KV))
        for p in range(2):
            sl = slice(p * LANES, (p + 1) * LANES)
            x = _dot_nt(qs[p], _mxu(kn[:, sl])) + head_rows(tb_ref[0], p) + madd
            update(p, x, lambda pe, sl=sl: _dot(pe, _mxu(vn[:, sl])))
            out = acc_ref[p] / l_ref[p]
            o_ref[0, :, (2 * p) * LANES:(2 * p + 1) * LANES] = jnp.where(lo, out[0:1], out[2:3])
            o_ref[0, :, (2 * p + 1) * LANES:(2 * p + 2) * LANES] = jnp.where(lo, out[1:2], out[3:4])


def _sample_attention(page_table, tab_b, qs, scores, thr, cut, s_new, k_new, v_new,
                      cache_kt, cache_vt, pps):
    db, n_pages = page_table.shape
    page_spec = lambda r: pl.BlockSpec(
        (1, D_KV, PAGE), lambda b, j, pt: (pt[b, j * pps + r], 0, 0))
    per_seq = lambda shape: pl.BlockSpec((1,) + shape, lambda b, j, pt: (b,) + (0,) * len(shape))
    gs = pltpu.PrefetchScalarGridSpec(
        num_scalar_prefetch=1,
        grid=(db, n_pages // pps),
        in_specs=[pl.BlockSpec(tab_b.shape, lambda b, j, pt: (0, 0, 0)),
                  per_seq((2, A_HEADS, LANES)),
                  pl.BlockSpec((1, 1, pps * PAGE), lambda b, j, pt: (b, 0, j)),
                  per_seq((1, LANES)), per_seq((1, LANES)), per_seq((1, LANES)),
                  per_seq((1, D_KV)), per_seq((1, D_KV))]
        + [page_spec(r) for r in range(pps)] * 2,
        out_specs=per_seq((1, D_ATTN)),
        scratch_shapes=[pltpu.VMEM((2, A_HEADS, 1), F32),
                        pltpu.VMEM((2, A_HEADS, 1), F32),
                        pltpu.VMEM((2, A_HEADS, LANES), F32)],
    )
    return pl.pallas_call(
        functools.partial(_sample_attn_kernel, pps),
        grid_spec=gs,
        out_shape=jax.ShapeDtypeStruct((db, 1, D_ATTN), F32),
        compiler_params=pltpu.CompilerParams(
            dimension_semantics=("arbitrary", "arbitrary"), vmem_limit_bytes=VMEM_LIMIT),
        name="sample_attn",
    )(page_table, tab_b, qs, scores, thr, cut, s_new, k_new, v_new,
      *([cache_kt] * pps), *([cache_vt] * pps))


def _permute_heads(a, axis):
    shape = a.shape
    a = a.reshape(shape[:axis] + (A_HEADS, HEAD_DIM) + shape[axis + 1:])
    a = jnp.take(a, jnp.array(HEAD_PERM), axis=axis)
    return a.reshape(shape)


def _pack_w_in(w):
    d = w.shape[0]
    o = [0]
    for wd in (D_ATTN, D_KV, D_KV, D_ATTN, IDX_HEADS * IDX_DIM, IDX_HEADS, IDX_DIM,
               D_H, D_H, D_H, D_H):
        o.append(o[-1] + wd)
    a_q, a_k, a_v, a_g, i_q, i_w, i_k, h_q, h_f, h_i, h_g = (
        w[:, o[n]:o[n + 1]] for n in range(11))
    z = lambda n: jnp.zeros((d, n), w.dtype)
    packed = jnp.concatenate(
        [_permute_heads(a_q, 1), a_k, a_v, _permute_heads(a_g, 1), i_q, i_k, i_k,
         z(IDX_DIM), i_w, z(LANES - IDX_DIM - IDX_HEADS), h_q, h_f, h_i, h_g], axis=1)
    assert packed.shape[1] == C_END
    return packed.astype(MXU_DTYPE)


def kernel(x_prompt, x_sample, cache_k, cache_v, cache_ik, state_hgrn, page_table, rel_bias,
           ln_g, w_in, q_norm_g, k_norm_g, hgrn_lb, hgrn_norm_g, w_out):
    b, s, d = x_prompt.shape
    db, dt, _ = x_sample.shape
    depth, n_pool = cache_k.shape[:2]
    n_pages = page_table.shape[1]
    past = n_pages * PAGE
    assert depth == 1 and dt == 1 and hgrn_lb.shape[0] == 2
    assert s % HG_BLK == 0 and s % FAR_CHUNK == 0 and s >= 4 * TOPK and past >= 4 * TOPK
    pps = min(MAX_PAGES_PER_STEP, n_pages)
    assert n_pages % pps == 0 and past % K_CHUNK == 0 and db % 8 == 0
    assert N_BUCKETS // 2 + int(math.log((PAGE + 1) / (N_BUCKETS // 2))
                                / math.log(MAX_DISTANCE / (N_BUCKETS // 2))
                                * (N_BUCKETS - N_BUCKETS // 2)) >= N_BUCKETS - 1

    w_pack = _pack_w_in(w_in[0])
    w_out_pack = jnp.concatenate(
        [_permute_heads(w_out[0][:D_ATTN], 0), w_out[0][D_ATTN:]], axis=0).astype(MXU_DTYPE)
    qg = jnp.tile(q_norm_g[0], A_HEADS)[None]
    kg = jnp.tile(k_norm_g[0], A_KV_HEADS)[None]
    hng = jnp.tile(hgrn_norm_g[0], H_HEADS)[None]
    bd = _block_diag(D_ATTN, HEAD_DIM, 1.0 / HEAD_DIM, MXU_DTYPE)
    lng = ln_g[0][None]

    pp = _project(x_prompt.reshape(b * s, d), lng, w_pack, qg, kg, hgrn_lb, bd, 256)
    r3 = lambda a: a.reshape(b, s, a.shape[-1])
    lp = -(-(s + Q_BLK) // K_CHUNK) * K_CHUNK
    padk = lambda a: jnp.pad(r3(a), ((0, 0), (Q_BLK, lp - s - Q_BLK), (0, 0)))
    tr = lambda a: a.transpose(0, 2, 1)
    iw_p = r3(pp["ikw"])[:, :, IDX_DIM:IDX_DIM + IDX_HEADS]
    vt = tr(padk(pp["vb"]))
    ones = jnp.ones((b, V_ROWS - LANES, lp), vt.dtype)
    vtp = jnp.concatenate([vt[:, :LANES], ones, vt[:, LANES:], ones], axis=1)
    a_out = _attn_prompt(rel_bias, _bias_tiles(rel_bias), tr(r3(pp["q"])), tr(r3(pp["iq"])),
                         tr(iw_p), padk(pp["kb"]), vtp, padk(pp["ik2"]))
    h_out, st = _hgrn_prompt(pp["hq"], pp["hg"], pp["hk"], pp["hv"], b, s)
    y_prompt = _merge(x_prompt.reshape(b * s, d), a_out.reshape(b * s, D_ATTN), pp["sga"],
                      h_out.reshape(b * s, D_H), pp["sgh"], hng, bd, w_out_pack, 256)
    st = st.reshape(b, D_H // LANES, 2, H_KEY, 2, H_KEY)
    s_prompt = jnp.stack([st[:, :, e, :, e, :] for e in range(2)], axis=2)
    s_prompt = s_prompt.reshape(b, H_HEADS, H_KEY, H_KEY).transpose(0, 1, 3, 2)

    sp = _project(x_sample.reshape(db, d), lng, w_pack, qg, kg, hgrn_lb, bd, db)
    ik_s = sp["ikw"][:, :IDX_DIM]
    iw_s = sp["ikw"][:, IDX_DIM:IDX_DIM + IDX_HEADS]
    scores, s_new = _sample_scores(
        page_table, sp["iq"].reshape(db, IDX_HEADS, IDX_DIM), iw_s.reshape(db, IDX_HEADS, 1),
        ik_s.reshape(db, 1, IDX_DIM), cache_ik[0].transpose(0, 2, 1), pps)
    thr, cut = _sample_threshold(scores.reshape(db, past).T, s_new[:, :, 0].T)
    lane_b = lambda a: jnp.broadcast_to(a.reshape(db, 1, 1), (db, 1, LANES))
    qg4 = sp["q"].reshape(db, 2, 2, 2, HEAD_DIM)
    rows = []
    for p in range(2):
        g0lo, g0hi = qg4[:, p, 0, 0], qg4[:, p, 0, 1]
        g1lo, g1hi = qg4[:, p, 1, 0], qg4[:, p, 1, 1]
        z1 = jnp.zeros_like(g0lo)
        pr = jnp.stack([jnp.concatenate([g0lo, z1], -1), jnp.concatenate([g1lo, z1], -1),
                        jnp.concatenate([z1, g0hi], -1), jnp.concatenate([z1, g1hi], -1)]
                       + [jnp.zeros((db, LANES), sp["q"].dtype)] * 4, axis=1)
        rows.append(pr)
    qs = jnp.stack(rows, axis=1)
    tab_b = jnp.broadcast_to(rel_bias[:, :, None] * LOG2E, (N_BUCKETS, A_HEADS, LANES))
    page_t = lambda c: c[0].transpose(0, 2, 3, 1).reshape(n_pool, D_KV, PAGE)
    a_out_s = _sample_attention(
        page_table, tab_b, qs, scores, lane_b(thr), lane_b(cut), s_new,
        sp["kf"].reshape(db, 1, D_KV), sp["vf"].reshape(db, 1, D_KV),
        page_t(cache_k), page_t(cache_v), pps)
    unpair = lambda a: a.transpose(1, 0, 2).reshape(db, D_H)
    h_out_s, s_sample = _hgrn_sample(unpair(sp["hq"]), unpair(sp["hg"]), unpair(sp["hk"]),
                                     unpair(sp["hv"]),
                                     state_hgrn[0].transpose(1, 2, 3, 0))
    s_sample = s_sample.transpose(3, 0, 1, 2)
    y_sample = _merge(x_sample.reshape(db, d), a_out_s.reshape(db, D_ATTN), sp["sga"],
                      h_out_s.reshape(db, D_H), sp["sgh"], hng, bd, w_out_pack, db)

    kv5 = lambda a, n: a.reshape(1, n, -1, A_KV_HEADS, HEAD_DIM)
    return (y_prompt.reshape(b, s, d), y_sample.reshape(db, 1, d),
            kv5(pp["kf"], b), kv5(pp["vf"], b),
            pp["ikw"][:, :IDX_DIM].reshape(1, b, s, IDX_DIM), s_prompt[None],
            kv5(sp["kf"], db), kv5(sp["vf"], db),
            ik_s.reshape(1, db, 1, IDX_DIM), s_sample[None])
```

```python
import functools
import math

import jax
import jax.numpy as jnp
from jax import lax
from jax.experimental import pallas as pl
from jax.experimental.pallas import tpu as pltpu

F32 = jnp.float32
BF16 = jnp.bfloat16
MXU_DTYPE = BF16

HEAD_DIM = 64
A_HEADS = 8
A_KV_HEADS = 4
D_ATTN = A_HEADS * HEAD_DIM
D_KV = A_KV_HEADS * HEAD_DIM
IDX_HEADS = 8
IDX_DIM = 64
H_HEADS = 8
H_KEY = 64
D_H = H_HEADS * H_KEY
TOPK = 256
PAGE = 128
N_BUCKETS = 32
MAX_DISTANCE = 128
EPS = 1e-6
LANES = 128
Q_BLK = 128
K_CHUNK = 512
FAR_CHUNK = 1024
HG_BLK = 256
HG_CHUNK = 16
V_ROWS = LANES + 16
MAX_PAGES_PER_STEP = 32
VMEM_LIMIT = 48 * 1024 * 1024
NEG_INF = float("-inf")
LOG2E = math.log2(math.e)
KEY_NEG_INF = -2139095041

C_AQ, C_AK, C_AV, C_AG, C_IQ, C_IK, C_IW, C_HQ, C_HF, C_HI, C_HG, C_END = (
    0, 512, 768, 1024, 1536, 2048, 2176, 2304, 2816, 3328, 3840, 4352)
HEAD_PERM = (0, 2, 1, 3, 4, 6, 5, 7)


def _mxu(x):
    return x.astype(MXU_DTYPE)


def _dot(a, b):
    return jnp.dot(a, b, preferred_element_type=F32)


def _dot_nt(a, b):
    return lax.dot_general(a, b, (((1,), (1,)), ((), ())), preferred_element_type=F32)


def _dot_tn(a, b):
    return lax.dot_general(a, b, (((0,), (0,)), ((), ())), preferred_element_type=F32)


def _split_dot(x, w, parts):
    acc = None
    r = x
    for _ in range(parts):
        p = _mxu(r)
        t = _dot(p, w)
        acc = t if acc is None else acc + t
        r = r - p.astype(F32)
    return acc


def _sigmoid(x):
    return 1.0 / (1.0 + jnp.exp(-x))


def _silu(x):
    return x * _sigmoid(x)


def _t5_bucket(n):
    n = jnp.maximum(n, 0)
    max_exact = N_BUCKETS // 2
    nf = jnp.maximum(n, 1).astype(F32)
    large = max_exact + jnp.floor(jnp.log(nf / max_exact) / math.log(MAX_DISTANCE / max_exact)
                                  * (N_BUCKETS - max_exact)).astype(jnp.int32)
    large = jnp.minimum(large, N_BUCKETS - 1)
    return jnp.where(n < max_exact, n, large)


def _block_diag(n, blk, val, dtype):
    r = lax.broadcasted_iota(jnp.int32, (n, n), 0) // blk
    c = lax.broadcasted_iota(jnp.int32, (n, n), 1) // blk
    return jnp.where(r == c, val, 0.0).astype(dtype)


def _proj_kernel(x_ref, lng_ref, w_ref, qg_ref, kg_ref, lb_ref, bd_ref,
                 qt_ref, iqt_ref, iwt_ref, kf_ref, kb_ref, vf_ref, vb_ref, ikw_ref, ik2_ref,
                 sga_ref, hq_ref, hg_ref, hk_ref, hv_ref, sgh_ref):
    x = x_ref[...]
    ms = jnp.mean(x * x, axis=-1, keepdims=True)
    xb = _mxu(x * lax.rsqrt(ms + EPS) * lng_ref[...])

    def seg(a, b):
        return _dot(xb, w_ref[:, a:b])

    bd = bd_ref[...]

    aq = seg(C_AQ, C_AK)
    msq = _split_dot(aq * aq, bd, 2)
    qt_ref[...] = (aq * lax.rsqrt(msq + EPS) * qg_ref[...]
                   * (HEAD_DIM ** -0.5 * LOG2E)).T.astype(qt_ref.dtype)

    ak = seg(C_AK, C_AV)
    msk = _split_dot(ak * ak, bd[:D_KV, :D_KV], 2)
    k = ak * lax.rsqrt(msk + EPS) * kg_ref[...]
    kf_ref[...] = k
    kb_ref[...] = k.astype(kb_ref.dtype)

    v = seg(C_AV, C_AG)
    vf_ref[...] = v
    vb_ref[...] = v.astype(vb_ref.dtype)

    sga_ref[...] = _silu(seg(C_AG, C_IQ))
    iqt_ref[...] = seg(C_IQ, C_IK).T.astype(iqt_ref.dtype)

    ikk = seg(C_IK, C_IW)
    iww = seg(C_IW, C_HQ) * (IDX_HEADS ** -0.5 * IDX_DIM ** -0.5)
    ik2_ref[...] = ikk.astype(ik2_ref.dtype)
    iwt_ref[...] = iww.T[IDX_DIM:IDX_DIM + IDX_HEADS]
    lane = lax.broadcasted_iota(jnp.int32, ikk.shape, 1)
    ikw_ref[...] = jnp.where(lane < IDX_DIM, ikk, iww)

    def put_pairs(ref, val):
        for p in range(D_H // LANES):
            ref[p] = val[:, p * LANES:(p + 1) * LANES]

    put_pairs(hq_ref, _silu(seg(C_HQ, C_HF)) * H_KEY ** -0.5)
    lbp = lb_ref[...]
    mx = jnp.max(lbp, axis=0, keepdims=True)
    e = jnp.exp(lbp - mx)
    lb = e[0:1] / jnp.sum(e, axis=0, keepdims=True)
    f = lb + (1.0 - lb) * _sigmoid(seg(C_HF, C_HI))
    put_pairs(hg_ref, jnp.log(f))
    put_pairs(hk_ref, 1.0 - f)
    put_pairs(hv_ref, seg(C_HI, C_HG))
    sgh_ref[...] = _silu(seg(C_HG, C_END))


def _project(x2d, ln_g, w_pack, qg, kg, lbp, bd, tm):
    n, d = x2d.shape
    row = lambda w: pl.BlockSpec((tm, w), lambda i: (i, 0))
    full = lambda a: pl.BlockSpec(a.shape, lambda i: (0,) * a.ndim)
    outs = [("qt", -D_ATTN, MXU_DTYPE), ("iqt", -D_ATTN, MXU_DTYPE), ("iwt", -IDX_HEADS, F32),
            ("kf", D_KV, F32),
            ("kb", D_KV, MXU_DTYPE), ("vf", D_KV, F32), ("vb", D_KV, MXU_DTYPE),
            ("ikw", LANES, F32), ("ik2", LANES, MXU_DTYPE), ("sga", D_ATTN, F32),
            ("hq", 0, F32), ("hg", 0, F32), ("hk", 0, F32), ("hv", 0, F32),
            ("sgh", D_H, F32)]
    n_pairs = D_H // LANES

    def spec(w):
        if w > 0:
            return row(w), (n, w)
        if w < 0:
            return pl.BlockSpec((-w, tm), lambda i: (0, i)), (-w, n)
        return pl.BlockSpec((n_pairs, tm, LANES), lambda i: (0, i, 0)), (n_pairs, n, LANES)

    res = pl.pallas_call(
        _proj_kernel,
        grid=(n // tm,),
        in_specs=[row(d), full(ln_g), full(w_pack), full(qg), full(kg), full(lbp), full(bd)],
        out_specs=[spec(w)[0] for _, w, _ in outs],
        out_shape=[jax.ShapeDtypeStruct(spec(w)[1], dt) for _, w, dt in outs],
        compiler_params=pltpu.CompilerParams(
            dimension_semantics=("arbitrary",), vmem_limit_bytes=VMEM_LIMIT),
        name="proj",
    )(x2d, ln_g, w_pack, qg, kg, lbp, bd)
    return dict(zip([o[0] for o in outs], res))


def _bias_tile_kernel(tab_ref, o_ref):
    j = lax.broadcasted_iota(jnp.int32, (2 * Q_BLK, Q_BLK), 0)
    r = lax.broadcasted_iota(jnp.int32, (2 * Q_BLK, Q_BLK), 1)
    bucket = _t5_bucket(Q_BLK + r - j)
    for h in range(A_HEADS):
        acc = jnp.zeros((2 * Q_BLK, Q_BLK), F32)
        for b in range(N_BUCKETS):
            acc = jnp.where(bucket == b, tab_ref[b, h] * LOG2E, acc)
        o_ref[h] = acc


def _bias_tiles(rel_bias):
    return pl.pallas_call(
        _bias_tile_kernel,
        in_specs=[pl.BlockSpec(memory_space=pltpu.SMEM)],
        out_shape=jax.ShapeDtypeStruct((A_HEADS, 2 * Q_BLK, Q_BLK), F32),
        name="bias_tiles",
    )(rel_bias)


def _key_to_f32(key):
    bits = key ^ ((key >> 31) & jnp.int32(0x7FFFFFFF))
    return lax.bitcast_convert_type(bits, F32)


def _topk_cut(sc_ref, nch, cut_ref, extra=None):
    ncol = sc_ref.shape[1]
    nrows = nch * K_CHUNK
    nbits = int(sc_ref.shape[0]).bit_length()
    groups = K_CHUNK // 8

    def count(pred):
        def body(c, acc):
            r0 = pl.multiple_of(c * K_CHUNK, K_CHUNK)
            hit = jnp.where(pred(sc_ref[pl.ds(r0, K_CHUNK), :], r0), 1, 0).astype(jnp.int32)
            return acc + jnp.sum(hit.reshape(groups, 8, ncol), axis=0, dtype=jnp.int32)

        acc = lax.fori_loop(0, nch, body, jnp.zeros((8, ncol), jnp.int32))
        cnt = jnp.sum(acc, axis=0, keepdims=True, dtype=jnp.int32)
        if extra is not None:
            cnt = cnt + jnp.where(pred(extra, nrows), 1, 0).astype(jnp.int32)
        return cnt

    def bit_body(it, carry):
        key, n_ge = carry
        cand = key + jnp.left_shift(jnp.int32(1), 31 - it)
        cf = _key_to_f32(cand)
        cnt = count(lambda v, r0: v >= cf)
        ok = cnt >= TOPK
        return jnp.where(ok, cand, key), jnp.where(ok, cnt, n_ge)

    key0 = jnp.full((1, ncol), jnp.iinfo(jnp.int32).min, jnp.int32)
    key, n_ge = lax.fori_loop(0, 32, bit_body, (key0, jnp.full((1, ncol), TOPK, jnp.int32)))
    below = key < KEY_NEG_INF
    thr = _key_to_f32(jnp.maximum(key, KEY_NEG_INF))
    cut_ref[...] = jnp.full((1, ncol), jnp.iinfo(jnp.int32).max, jnp.int32)

    @pl.when(jnp.max(jnp.where(below, TOPK + 1, n_ge)) > TOPK)
    def _():
        need = TOPK - count(lambda v, r0: v > thr)

        def row_of(v, r0):
            return r0 + lax.broadcasted_iota(jnp.int32, v.shape, 0)

        def idx_body(it, x):
            cand = x + jnp.left_shift(jnp.int32(1), nbits - 1 - it)
            g = count(lambda v, r0: (v == thr) & (row_of(v, r0) < cand))
            return jnp.where(g < need, cand, x)

        cut_ref[...] = lax.fori_loop(0, nbits, idx_body, jnp.zeros((1, ncol), jnp.int32))

    return thr


def _in_topk(sc, col, thr, cut):
    return (sc > thr) | ((sc == thr) & (col <= cut))


def _row_halves(x):
    half = x.shape[0] // 2
    zero = jnp.zeros((half, x.shape[1]), x.dtype)
    return (jnp.concatenate([x[:half], zero], axis=0), jnp.concatenate([zero, x[half:]], axis=0))


def _attn_prompt_kernel(tab_ref, qt_ref, iqt_ref, iwt_ref, kb_ref, vt_ref, ik2_ref, bt_ref,
                        o_ref, sc_ref, cut_ref, m_ref, acc_ref):
    i = pl.program_id(1)
    t_pos = i * Q_BLK + lax.broadcasted_iota(jnp.int32, (1, Q_BLK), 1)

    @pl.when((pl.program_id(0) == 0) & (i == 0))
    def _():
        sc_ref[...] = jnp.full(sc_ref.shape, NEG_INF, F32)

    iqt = iqt_ref[...]
    cols = []
    for g in range(D_ATTN // LANES):
        cols.extend(_row_halves(iqt[g * LANES:(g + 1) * LANES]))
    iq_stack = jnp.concatenate(cols, axis=1)
    iw = iwt_ref[...]
    nch = (i + 5) // 4

    def score_body(c, carry):
        r0 = pl.multiple_of(c * K_CHUNK, K_CHUNK)
        s_all = _dot(ik2_ref[0, pl.ds(r0, K_CHUNK), :], iq_stack)
        sc = jnp.zeros((K_CHUNK, Q_BLK), F32)
        for h in range(IDX_HEADS):
            sc = sc + jnp.maximum(s_all[:, h * Q_BLK:(h + 1) * Q_BLK], 0.0) * iw[h:h + 1]
        s_glob = r0 - Q_BLK + lax.broadcasted_iota(jnp.int32, (K_CHUNK, Q_BLK), 0)
        valid = (s_glob >= 0) & (s_glob <= t_pos)
        sc_ref[pl.ds(r0, K_CHUNK), :] = jnp.where(valid, sc, NEG_INF)
        return carry

    lax.fori_loop(0, nch, score_body, 0)
    thr = _topk_cut(sc_ref, nch, cut_ref)
    cut = cut_ref[...]

    m_ref[...] = jnp.full(m_ref.shape, NEG_INF, F32)
    acc_ref[...] = jnp.zeros(acc_ref.shape, F32)

    qt = qt_ref[...]
    q_pairs = []
    for p in range(2):
        g0 = _row_halves(qt[(2 * p) * LANES:(2 * p + 1) * LANES])
        g1 = _row_halves(qt[(2 * p + 1) * LANES:(2 * p + 2) * LANES])
        q_pairs.append(jnp.concatenate([g0[0], g1[0], g0[1], g1[1]], axis=1))

    def attend(r0, width, valid, bias_fn, const_fn):
        madd = jnp.where(valid, 0.0, NEG_INF)
        for p in range(2):
            kc = kb_ref[0, pl.ds(r0, width), p * LANES:(p + 1) * LANES]
            vc = vt_ref[0, p * V_ROWS:(p + 1) * V_ROWS, pl.ds(r0, width)]
            st = _dot(kc, q_pairs[p])
            ps, alphas = [], []
            for a in range(4):
                h = 4 * p + a
                x = st[:, a * Q_BLK:(a + 1) * Q_BLK] + madd
                bias = bias_fn(h)
                if bias is not None:
                    x = x + bias
                c = const_fn(h)
                m_old = m_ref[h:h + 1]
                m_new = jnp.maximum(m_old, jnp.max(x, axis=0, keepdims=True) + c)
                m_safe = jnp.where(m_new == NEG_INF, 0.0, m_new)
                m_ref[h:h + 1] = m_new
                ps.append(_mxu(jnp.exp2(x - (m_safe - c))))
                alphas.append(jnp.exp2(m_old - m_safe))
            pv = _dot(vc, jnp.concatenate(ps, axis=1))
            acc_ref[p] = acc_ref[p] * jnp.concatenate(alphas, axis=1) + pv

    def far_body(c, carry):
        r0 = pl.multiple_of(c * FAR_CHUNK, FAR_CHUNK)
        row = r0 + lax.broadcasted_iota(jnp.int32, (FAR_CHUNK, Q_BLK), 0)
        valid = ((row >= Q_BLK) & (row < i * Q_BLK)
                 & _in_topk(sc_ref[pl.ds(r0, FAR_CHUNK), :], row, thr, cut))
        attend(r0, FAR_CHUNK, valid, lambda h: None, lambda h: tab_ref[N_BUCKETS - 1, h] * LOG2E)
        return carry

    lax.fori_loop(0, (i * Q_BLK + FAR_CHUNK - 1) // FAR_CHUNK, far_body, 0)

    r0 = pl.multiple_of(i * Q_BLK, Q_BLK)
    row = r0 + lax.broadcasted_iota(jnp.int32, (2 * Q_BLK, Q_BLK), 0)
    valid = ((row >= Q_BLK) & (row - Q_BLK <= t_pos)
             & _in_topk(sc_ref[pl.ds(r0, 2 * Q_BLK), :], row, thr, cut))
    attend(r0, 2 * Q_BLK, valid, lambda h: bt_ref[h], lambda h: 0.0)

    top = lax.broadcasted_iota(jnp.int32, (LANES, Q_BLK), 0) < HEAD_DIM
    for p in range(2):
        acc = acc_ref[p]
        outs = [acc[:LANES, a * Q_BLK:(a + 1) * Q_BLK] / acc[LANES:LANES + 1, a * Q_BLK:(a + 1) * Q_BLK]
                for a in range(4)]
        o_ref[0, :, (2 * p) * LANES:(2 * p + 1) * LANES] = jnp.where(top, outs[0], outs[2]).T
        o_ref[0, :, (2 * p + 1) * LANES:(2 * p + 2) * LANES] = jnp.where(top, outs[1], outs[3]).T


def _attn_prompt(rel_bias, bias_tiles, qt, iqt, iwt, kbp, vtp, ik2p, b, s):
    nq = s // Q_BLK
    lp = kbp.shape[1]
    blk_t = lambda r: pl.BlockSpec((r, Q_BLK), lambda bi, i: (0, bi * nq + i))
    return pl.pallas_call(
        _attn_prompt_kernel,
        grid=(b, s // Q_BLK),
        in_specs=[pl.BlockSpec(memory_space=pltpu.SMEM),
                  blk_t(D_ATTN), blk_t(D_ATTN), blk_t(IDX_HEADS),
                  pl.BlockSpec((1, lp, D_KV), lambda bi, i: (bi, 0, 0)),
                  pl.BlockSpec((1, 2 * V_ROWS, lp), lambda bi, i: (bi, 0, 0)),
                  pl.BlockSpec((1, lp, LANES), lambda bi, i: (bi, 0, 0)),
                  pl.BlockSpec(bias_tiles.shape, lambda bi, i: (0, 0, 0))],
        out_specs=pl.BlockSpec((1, Q_BLK, D_ATTN), lambda bi, i: (bi, i, 0)),
        out_shape=jax.ShapeDtypeStruct((b, s, D_ATTN), F32),
        scratch_shapes=[pltpu.VMEM((lp, Q_BLK), F32),
                        pltpu.VMEM((1, Q_BLK), jnp.int32),
                        pltpu.VMEM((A_HEADS, Q_BLK), F32),
                        pltpu.VMEM((2, V_ROWS, 4 * Q_BLK), F32)],
        compiler_params=pltpu.CompilerParams(
            dimension_semantics=("arbitrary", "arbitrary"), vmem_limit_bytes=VMEM_LIMIT),
        name="attn_prompt",
    )(rel_bias, qt, iqt, iwt, kbp, vtp, ik2p, bias_tiles)


def _hgrn_prompt_kernel(q_ref, g_ref, k_ref, v_ref, o_ref, st_ref, b_ref, oi_ref,
                        tq_ref, tk_ref, tv_ref, tb_ref):
    j = pl.program_id(1)
    n_pairs = D_H // LANES
    n_chunks = HG_BLK // HG_CHUNK

    @pl.when(j == 0)
    def _():
        st_ref[...] = jnp.zeros(st_ref.shape, F32)

    def lanes(ref, rows):
        return jnp.concatenate([ref[p, rows, :] for p in range(n_pairs)], axis=1)

    r = lax.broadcasted_iota(jnp.int32, (HG_BLK, HG_BLK), 0)
    c = lax.broadcasted_iota(jnp.int32, (HG_BLK, HG_BLK), 1)
    tri = jnp.where(c <= r, 1.0, 0.0).astype(MXU_DTYPE)
    bcum = _split_dot_left(tri, lanes(g_ref, slice(None)), 3)
    for p in range(n_pairs):
        b_ref[p] = bcum[:, p * LANES:(p + 1) * LANES]

    for t in range(HG_CHUNK):
        rows = pl.ds(t, n_chunks, stride=HG_CHUNK)
        tq_ref[t] = lanes(q_ref, rows)
        tk_ref[t] = lanes(k_ref, rows)
        tv_ref[t] = lanes(v_ref, rows)
        tb_ref[t] = lanes(b_ref, rows)

    half = D_H // 2
    bd_half = _block_diag(half, H_KEY, 1.0, MXU_DTYPE)
    for t in range(HG_CHUNK):
        qt = tq_ref[t]
        bt = tb_ref[t]
        parts = [qt * tk_ref[s] * jnp.exp(bt - tb_ref[s]) for s in range(t)]
        parts.append(qt * tk_ref[t])
        pr = _mxu(jnp.concatenate(parts, axis=0))
        rs = jnp.concatenate([_dot(pr[:, :half], bd_half), _dot(pr[:, half:], bd_half)], axis=1)
        acc = rs[0:n_chunks] * tv_ref[0]
        for s in range(1, t + 1):
            acc = acc + rs[s * n_chunks:(s + 1) * n_chunks] * tv_ref[s]
        for p in range(n_pairs):
            oi_ref[p, pl.ds(t, n_chunks, stride=HG_CHUNK), :] = acc[:, p * LANES:(p + 1) * LANES]

    lane_r = lax.broadcasted_iota(jnp.int32, (LANES, LANES), 0) // H_KEY
    lane_c = lax.broadcasted_iota(jnp.int32, (LANES, LANES), 1) // H_KEY
    same_head = lane_r == lane_c
    for cch in range(n_chunks):
        rows = slice(cch * HG_CHUNK, (cch + 1) * HG_CHUNK)
        last = slice((cch + 1) * HG_CHUNK - 1, (cch + 1) * HG_CHUNK)
        prev = slice(cch * HG_CHUNK - 1, cch * HG_CHUNK)
        outs = []
        for p in range(n_pairs):
            bch = b_ref[p, rows, :]
            b0 = b_ref[p, prev, :] if cch > 0 else jnp.zeros((1, LANES), F32)
            bl = b_ref[p, last, :]
            qi = _mxu(q_ref[p, rows, :] * jnp.exp(bch - b0))
            kl = _mxu(k_ref[p, rows, :] * jnp.exp(bl - bch))
            st = st_ref[0, p]
            outs.append(oi_ref[p, rows, :] + _dot_nt(qi, _mxu(st)))
            ds = _dot_tn(_mxu(v_ref[p, rows, :]), kl)
            st_ref[0, p] = st * jnp.exp(bl - b0) + jnp.where(same_head, ds, 0.0)
        o_ref[0, rows, :] = jnp.concatenate(outs, axis=1)


def _split_dot_left(w, x, parts):
    acc = None
    r = x
    for _ in range(parts):
        p = _mxu(r)
        t = _dot(w, p)
        acc = t if acc is None else acc + t
        r = r - p.astype(F32)
    return acc


def _hgrn_prompt(hq, hg, hk, hv, b, s):
    n_pairs = D_H // LANES
    nblk = s // HG_BLK
    blk = pl.BlockSpec((n_pairs, HG_BLK, LANES), lambda bi, j: (0, bi * nblk + j, 0))
    return pl.pallas_call(
        _hgrn_prompt_kernel,
        grid=(b, nblk),
        in_specs=[blk, blk, blk, blk],
        out_specs=[pl.BlockSpec((1, HG_BLK, D_H), lambda bi, j: (bi, j, 0)),
                   pl.BlockSpec((1, n_pairs, LANES, LANES), lambda bi, j: (bi, 0, 0, 0))],
        out_shape=[jax.ShapeDtypeStruct((b, s, D_H), F32),
                   jax.ShapeDtypeStruct((b, n_pairs, LANES, LANES), F32)],
        scratch_shapes=[pltpu.VMEM((n_pairs, HG_BLK, LANES), F32)] * 2
        + [pltpu.VMEM((HG_CHUNK, HG_BLK // HG_CHUNK, D_H), F32)] * 4,
        compiler_params=pltpu.CompilerParams(
            dimension_semantics=("arbitrary", "arbitrary"), vmem_limit_bytes=VMEM_LIMIT),
        name="hgrn_prompt",
    )(hq, hg, hk, hv)


def _hgrn_sample_kernel(q_ref, g_ref, k_ref, v_ref, s_ref, o_ref, sn_ref):
    q = q_ref[...]
    e = jnp.exp(g_ref[...])
    k = k_ref[...]
    v = v_ref[...]
    o = jnp.zeros(v.shape, F32)
    for kk in range(H_KEY):
        sn = e[kk:kk + 1] * s_ref[0, kk] + k[kk:kk + 1] * v
        sn_ref[0, kk] = sn
        o = o + q[kk:kk + 1] * sn
    o_ref[...] = o


def _hgrn_sample(hq, hg, hk, hv, state_t):
    db = hq.shape[0]
    hspec = pl.BlockSpec((H_KEY, db), lambda h: (h, 0))
    sspec = pl.BlockSpec((1, H_KEY, H_KEY, db), lambda h: (h, 0, 0, 0))
    o_t, s_new = pl.pallas_call(
        _hgrn_sample_kernel,
        grid=(H_HEADS,),
        in_specs=[hspec, hspec, hspec, hspec, sspec],
        out_specs=[hspec, sspec],
        out_shape=[jax.ShapeDtypeStruct((D_H, db), F32),
                   jax.ShapeDtypeStruct(state_t.shape, F32)],
        compiler_params=pltpu.CompilerParams(
            dimension_semantics=("arbitrary",), vmem_limit_bytes=VMEM_LIMIT),
        name="hgrn_sample",
    )(hq.T, hg.T, hk.T, hv.T, state_t)
    return o_t.T, s_new


def _merge_kernel(x_ref, a_ref, sga_ref, h_ref, sgh_ref, hng_ref, bd_ref, w_ref, y_ref):
    h = h_ref[...]
    msq = _split_dot(h * h, bd_ref[...], 2)
    hn = h * lax.rsqrt(msq + EPS) * hng_ref[...]
    ma = _mxu(a_ref[...] * sga_ref[...])
    mh = _mxu(hn * sgh_ref[...])
    y_ref[...] = x_ref[...] + _dot(ma, w_ref[0:D_ATTN, :]) + _dot(mh, w_ref[D_ATTN:, :])


def _merge(x2d, a_out, sga, h_out, sgh, hng, bd, w_pack, tm):
    n, d = x2d.shape
    row = lambda w: pl.BlockSpec((tm, w), lambda i: (i, 0))
    full = lambda a: pl.BlockSpec(a.shape, lambda i: (0,) * a.ndim)
    return pl.pallas_call(
        _merge_kernel,
        grid=(n // tm,),
        in_specs=[row(d), row(D_ATTN), row(D_ATTN), row(D_H), row(D_H),
                  full(hng), full(bd), full(w_pack)],
        out_specs=row(d),
        out_shape=jax.ShapeDtypeStruct((n, d), F32),
        compiler_params=pltpu.CompilerParams(
            dimension_semantics=("arbitrary",), vmem_limit_bytes=VMEM_LIMIT),
        name="merge",
    )(x2d, a_out, sga, h_out, sgh, hng, bd, w_pack)


def _sample_score_kernel(pps, pt_ref, iqh_ref, iwc_ref, ikn_ref, *refs):
    ik_refs = refs[:pps]
    sc_ref, sn_ref = refs[pps:]
    iqh = iqh_ref[0]
    iwc = iwc_ref[0]
    ikt = jnp.concatenate([_mxu(r[0]) for r in ik_refs], axis=1)
    s = _dot(iqh, ikt)
    sc_ref[0] = jnp.sum(jnp.maximum(s, 0.0) * iwc, axis=0, keepdims=True)
    prod = iqh.astype(F32) * _mxu(ikn_ref[0]).astype(F32)
    s_new = jnp.sum(prod, axis=1, keepdims=True)
    s_new = jnp.sum(jnp.maximum(s_new, 0.0) * iwc, axis=0, keepdims=True)
    sn_ref[0] = jnp.broadcast_to(s_new, (1, LANES))


def _sample_scores(page_table, iqh, iwc, ikn, cache_ikt, pps):
    db, n_pages = page_table.shape
    page_spec = lambda r: pl.BlockSpec(
        (1, IDX_DIM, PAGE), lambda b, j, pt: (pt[b, j * pps + r], 0, 0))
    gs = pltpu.PrefetchScalarGridSpec(
        num_scalar_prefetch=1,
        grid=(db, n_pages // pps),
        in_specs=[pl.BlockSpec((1, IDX_HEADS, IDX_DIM), lambda b, j, pt: (b, 0, 0)),
                  pl.BlockSpec((1, IDX_HEADS, 1), lambda b, j, pt: (b, 0, 0)),
                  pl.BlockSpec((1, 1, IDX_DIM), lambda b, j, pt: (b, 0, 0))]
        + [page_spec(r) for r in range(pps)],
        out_specs=[pl.BlockSpec((1, 1, pps * PAGE), lambda b, j, pt: (b, 0, j)),
                   pl.BlockSpec((1, 1, LANES), lambda b, j, pt: (b, 0, 0))],
    )
    return pl.pallas_call(
        functools.partial(_sample_score_kernel, pps),
        grid_spec=gs,
        out_shape=[jax.ShapeDtypeStruct((db, 1, n_pages * PAGE), F32),
                   jax.ShapeDtypeStruct((db, 1, LANES), F32)],
        compiler_params=pltpu.CompilerParams(
            dimension_semantics=("arbitrary", "arbitrary"), vmem_limit_bytes=VMEM_LIMIT),
        name="sample_scores",
    )(page_table, iqh, iwc, ikn, *([cache_ikt] * pps))


def _sample_thr_kernel(sc_ref, sn_ref, thr_ref, cut_ref):
    past = sc_ref.shape[0]
    thr_ref[...] = _topk_cut(sc_ref, past // K_CHUNK, cut_ref, extra=sn_ref[...])


def _sample_threshold(scores_t, s_new_t):
    db = scores_t.shape[1]
    return pl.pallas_call(
        _sample_thr_kernel,
        out_shape=[jax.ShapeDtypeStruct((1, db), F32),
                   jax.ShapeDtypeStruct((1, db), jnp.int32)],
        compiler_params=pltpu.CompilerParams(vmem_limit_bytes=VMEM_LIMIT),
        name="sample_threshold",
    )(scores_t, s_new_t)


def _sample_attn_kernel(pps, pt_ref, tb_ref, qs_ref, sc_ref, thr_ref, cut_ref, sn_ref,
                        kn_ref, vn_ref, *refs):
    k_refs = refs[:pps]
    v_refs = refs[pps:2 * pps]
    o_ref, m_ref, l_ref, acc_ref = refs[2 * pps:]
    j = pl.program_id(1)
    nj = pl.num_programs(1)
    n = pps * PAGE
    past = nj * n

    @pl.when(j == 0)
    def _():
        m_ref[...] = jnp.full(m_ref.shape, NEG_INF, F32)
        l_ref[...] = jnp.zeros(l_ref.shape, F32)
        acc_ref[...] = jnp.zeros(acc_ref.shape, F32)

    thr = thr_ref[0, :, 0:1]
    cut = cut_ref[0, :, 0:1]
    qs = qs_ref[0]
    lo = lax.broadcasted_iota(jnp.int32, (1, LANES), 1) < HEAD_DIM

    def update(x, pv_fn):
        m_old = m_ref[...]
        m_new = jnp.maximum(m_old, jnp.max(x, axis=1, keepdims=True))
        m_safe = jnp.where(m_new == NEG_INF, 0.0, m_new)
        alpha = jnp.exp2(m_old - m_safe)
        pe = jnp.exp2(x - m_safe)
        l_ref[...] = alpha * l_ref[...] + jnp.sum(pe, axis=1, keepdims=True)
        m_ref[...] = m_new
        acc_ref[...] = alpha * acc_ref[...] + pv_fn(_mxu(pe))

    def head_rows(b8):
        z = jnp.zeros((4, b8.shape[1]), F32)
        return jnp.concatenate([b8[0:4], z, b8[4:8], z], axis=0)

    s_idx = j * n + lax.broadcasted_iota(jnp.int32, (1, n), 1)
    s_last = s_idx[:, n - PAGE:]
    bucket = _t5_bucket(past - s_last)
    near = jnp.zeros((A_HEADS, PAGE), F32)
    for b in range(N_BUCKETS):
        near = jnp.where(bucket == b, tb_ref[b], near)
    far = tb_ref[N_BUCKETS - 1]
    bias8 = jnp.concatenate([far] * (pps - 1) + [jnp.where(j == nj - 1, near, far)], axis=1)
    maskadd = jnp.where(_in_topk(sc_ref[0], s_idx, thr, cut), 0.0, NEG_INF)
    kt = jnp.concatenate([_mxu(r[0]) for r in k_refs], axis=1)
    vt = jnp.concatenate([_mxu(r[0]) for r in v_refs], axis=1)
    update(_dot(qs, kt) + head_rows(bias8) + maskadd, lambda pe: _dot_nt(pe, vt))

    @pl.when(j == nj - 1)
    def _():
        valid = (_in_topk(sn_ref[0], past, thr, cut)
                 & (lax.broadcasted_iota(jnp.int32, (1, LANES), 1) == 0))
        madd = jnp.where(valid, 0.0, NEG_INF)
        kn = _mxu(jnp.broadcast_to(kn_ref[0], (LANES, D_KV)))
        vn = _mxu(jnp.broadcast_to(vn_ref[0], (LANES, D_KV)))
        update(_dot_nt(qs, kn) + head_rows(tb_ref[0]) + madd, lambda pe: _dot(pe, vn))
        out = acc_ref[...] / l_ref[...]
        for p in range(2):
            o = out[8 * p:8 * p + 4, p * LANES:(p + 1) * LANES]
            o_ref[0, :, (2 * p) * LANES:(2 * p + 1) * LANES] = jnp.where(lo, o[0:1], o[2:3])
            o_ref[0, :, (2 * p + 1) * LANES:(2 * p + 2) * LANES] = jnp.where(lo, o[1:2], o[3:4])


def _sample_attention(page_table, tab_b, qs, scores, thr, cut, s_new, k_new, v_new,
                      cache_kt, cache_vt, pps):
    db, n_pages = page_table.shape
    page_spec = lambda r: pl.BlockSpec(
        (1, D_KV, PAGE), lambda b, j, pt: (pt[b, j * pps + r], 0, 0))
    per_seq = lambda shape: pl.BlockSpec((1,) + shape, lambda b, j, pt: (b,) + (0,) * len(shape))
    gs = pltpu.PrefetchScalarGridSpec(
        num_scalar_prefetch=1,
        grid=(db, n_pages // pps),
        in_specs=[pl.BlockSpec(tab_b.shape, lambda b, j, pt: (0, 0, 0)),
                  per_seq((2 * A_HEADS, D_KV)),
                  pl.BlockSpec((1, 1, pps * PAGE), lambda b, j, pt: (b, 0, j)),
                  per_seq((1, LANES)), per_seq((1, LANES)), per_seq((1, LANES)),
                  per_seq((1, D_KV)), per_seq((1, D_KV))]
        + [page_spec(r) for r in range(pps)] * 2,
        out_specs=per_seq((1, D_ATTN)),
        scratch_shapes=[pltpu.VMEM((2 * A_HEADS, 1), F32),
                        pltpu.VMEM((2 * A_HEADS, 1), F32),
                        pltpu.VMEM((2 * A_HEADS, D_KV), F32)],
    )
    return pl.pallas_call(
        functools.partial(_sample_attn_kernel, pps),
        grid_spec=gs,
        out_shape=jax.ShapeDtypeStruct((db, 1, D_ATTN), F32),
        compiler_params=pltpu.CompilerParams(
            dimension_semantics=("arbitrary", "arbitrary"), vmem_limit_bytes=VMEM_LIMIT),
        name="sample_attn",
    )(page_table, tab_b, qs, scores, thr, cut, s_new, k_new, v_new,
      *([cache_kt] * pps), *([cache_vt] * pps))


def _permute_heads(a, axis):
    shape = a.shape
    a = a.reshape(shape[:axis] + (A_HEADS, HEAD_DIM) + shape[axis + 1:])
    a = jnp.take(a, jnp.array(HEAD_PERM), axis=axis)
    return a.reshape(shape)


def _pack_w_in(w):
    d = w.shape[0]
    o = [0]
    for wd in (D_ATTN, D_KV, D_KV, D_ATTN, IDX_HEADS * IDX_DIM, IDX_HEADS, IDX_DIM,
               D_H, D_H, D_H, D_H):
        o.append(o[-1] + wd)
    a_q, a_k, a_v, a_g, i_q, i_w, i_k, h_q, h_f, h_i, h_g = (
        w[:, o[n]:o[n + 1]] for n in range(11))
    z = lambda n: jnp.zeros((d, n), w.dtype)
    packed = jnp.concatenate(
        [_permute_heads(a_q, 1), a_k, a_v, _permute_heads(a_g, 1), i_q, i_k, i_k,
         z(IDX_DIM), i_w, z(LANES - IDX_DIM - IDX_HEADS), h_q, h_f, h_i, h_g], axis=1)
    assert packed.shape[1] == C_END
    return packed.astype(MXU_DTYPE)


def kernel(x_prompt, x_sample, cache_k, cache_v, cache_ik, state_hgrn, page_table, rel_bias,
           ln_g, w_in, q_norm_g, k_norm_g, hgrn_lb, hgrn_norm_g, w_out):
    b, s, d = x_prompt.shape
    db, dt, _ = x_sample.shape
    depth, n_pool = cache_k.shape[:2]
    n_pages = page_table.shape[1]
    past = n_pages * PAGE
    assert depth == 1 and dt == 1 and hgrn_lb.shape[0] == 2
    assert s % HG_BLK == 0 and s % FAR_CHUNK == 0 and s >= 4 * TOPK and past >= 4 * TOPK
    pps = min(MAX_PAGES_PER_STEP, n_pages)
    assert n_pages % pps == 0 and past % K_CHUNK == 0 and db % 8 == 0
    assert N_BUCKETS // 2 + int(math.log((PAGE + 1) / (N_BUCKETS // 2))
                                / math.log(MAX_DISTANCE / (N_BUCKETS // 2))
                                * (N_BUCKETS - N_BUCKETS // 2)) >= N_BUCKETS - 1

    w_pack = _pack_w_in(w_in[0])
    w_out_pack = jnp.concatenate(
        [_permute_heads(w_out[0][:D_ATTN], 0), w_out[0][D_ATTN:]], axis=0).astype(MXU_DTYPE)
    qg = jnp.tile(q_norm_g[0], A_HEADS)[None]
    kg = jnp.tile(k_norm_g[0], A_KV_HEADS)[None]
    hng = jnp.tile(hgrn_norm_g[0], H_HEADS)[None]
    bd = _block_diag(D_ATTN, HEAD_DIM, 1.0 / HEAD_DIM, MXU_DTYPE)
    lng = ln_g[0][None]

    pp = _project(x_prompt.reshape(b * s, d), lng, w_pack, qg, kg, hgrn_lb, bd, 256)
    r3 = lambda a: a.reshape(b, s, a.shape[-1])
    lp = -(-(s + Q_BLK) // K_CHUNK) * K_CHUNK
    padk = lambda a: jnp.pad(r3(a), ((0, 0), (Q_BLK, lp - s - Q_BLK), (0, 0)))
    vt = padk(pp["vb"]).transpose(0, 2, 1)
    ones = jnp.ones((b, V_ROWS - LANES, lp), vt.dtype)
    vtp = jnp.concatenate([vt[:, :LANES], ones, vt[:, LANES:], ones], axis=1)
    a_out = _attn_prompt(rel_bias, _bias_tiles(rel_bias), pp["qt"], pp["iqt"], pp["iwt"],
                         padk(pp["kb"]), vtp, padk(pp["ik2"]), b, s)
    h_out, st = _hgrn_prompt(pp["hq"], pp["hg"], pp["hk"], pp["hv"], b, s)
    y_prompt = _merge(x_prompt.reshape(b * s, d), a_out.reshape(b * s, D_ATTN), pp["sga"],
                      h_out.reshape(b * s, D_H), pp["sgh"], hng, bd, w_out_pack, 256)
    st = st.reshape(b, D_H // LANES, 2, H_KEY, 2, H_KEY)
    s_prompt = jnp.stack([st[:, :, e, :, e, :] for e in range(2)], axis=2)
    s_prompt = s_prompt.reshape(b, H_HEADS, H_KEY, H_KEY).transpose(0, 1, 3, 2)

    sp = _project(x_sample.reshape(db, d), lng, w_pack, qg, kg, hgrn_lb, bd, db)
    ik_s = sp["ikw"][:, :IDX_DIM]
    iw_s = sp["ikw"][:, IDX_DIM:IDX_DIM + IDX_HEADS]
    scores, s_new = _sample_scores(
        page_table, sp["iqt"].T.reshape(db, IDX_HEADS, IDX_DIM), iw_s.reshape(db, IDX_HEADS, 1),
        ik_s.reshape(db, 1, IDX_DIM), cache_ik[0].transpose(0, 2, 1),
        min(2 * MAX_PAGES_PER_STEP, n_pages))
    thr, cut = _sample_threshold(scores.reshape(db, past).T, s_new[:, :, 0].T)
    lane_b = lambda a: jnp.broadcast_to(a.reshape(db, 1, 1), (db, 1, LANES))
    q_s = sp["qt"].T
    qg4 = q_s.reshape(db, 2, 2, 2, HEAD_DIM)
    rows = []
    for p in range(2):
        g0lo, g0hi = qg4[:, p, 0, 0], qg4[:, p, 0, 1]
        g1lo, g1hi = qg4[:, p, 1, 0], qg4[:, p, 1, 1]
        z1 = jnp.zeros_like(g0lo)
        pr = jnp.stack([jnp.concatenate([g0lo, z1], -1), jnp.concatenate([g1lo, z1], -1),
                        jnp.concatenate([z1, g0hi], -1), jnp.concatenate([z1, g1hi], -1)]
                       + [jnp.zeros((db, LANES), q_s.dtype)] * 4, axis=1)
        rows.append(pr)
    zq = jnp.zeros_like(rows[0])
    qs = jnp.concatenate([jnp.concatenate([rows[0], zq], axis=-1),
                          jnp.concatenate([zq, rows[1]], axis=-1)], axis=1)
    tab_b = jnp.broadcast_to(rel_bias[:, :, None] * LOG2E, (N_BUCKETS, A_HEADS, LANES))
    page_t = lambda c: c[0].transpose(0, 2, 3, 1).reshape(n_pool, D_KV, PAGE)
    a_out_s = _sample_attention(
        page_table, tab_b, qs, scores, lane_b(thr), lane_b(cut), s_new,
        sp["kf"].reshape(db, 1, D_KV), sp["vf"].reshape(db, 1, D_KV),
        page_t(cache_k), page_t(cache_v), pps)
    unpair = lambda a: a.transpose(1, 0, 2).reshape(db, D_H)
    h_out_s, s_sample = _hgrn_sample(unpair(sp["hq"]), unpair(sp["hg"]), unpair(sp["hk"]),
                                     unpair(sp["hv"]),
                                     state_hgrn[0].transpose(1, 2, 3, 0))
    s_sample = s_sample.transpose(3, 0, 1, 2)
    y_sample = _merge(x_sample.reshape(db, d), a_out_s.reshape(db, D_ATTN), sp["sga"],
                      h_out_s.reshape(db, D_H), sp["sgh"], hng, bd, w_out_pack, db)

    kv5 = lambda a, n: a.reshape(1, n, -1, A_KV_HEADS, HEAD_DIM)
    return (y_prompt.reshape(b, s, d), y_sample.reshape(db, 1, d),
            kv5(pp["kf"], b), kv5(pp["vf"], b),
            pp["ikw"][:, :IDX_DIM].reshape(1, b, s, IDX_DIM), s_prompt[None],
            kv5(sp["kf"], db), kv5(sp["vf"], db),
            ik_s.reshape(1, db, 1, IDX_DIM), s_sample[None])
```

```python
import functools
import math

import jax
import jax.numpy as jnp
from jax import lax
from jax.experimental import pallas as pl
from jax.experimental.pallas import tpu as pltpu

F32 = jnp.float32
BF16 = jnp.bfloat16
MXU_DTYPE = BF16

HEAD_DIM = 64
A_HEADS = 8
A_KV_HEADS = 4
D_ATTN = A_HEADS * HEAD_DIM
D_KV = A_KV_HEADS * HEAD_DIM
IDX_HEADS = 8
IDX_DIM = 64
H_HEADS = 8
H_KEY = 64
D_H = H_HEADS * H_KEY
TOPK = 256
PAGE = 128
N_BUCKETS = 32
MAX_DISTANCE = 128
EPS = 1e-6
LANES = 128
Q_BLK = 256
K_CHUNK = 512
FAR_CHUNK = 1024
HG_BLK = 256
HG_CHUNK = 16
V_ROWS = LANES + 16
MAX_PAGES_PER_STEP = 32
VMEM_LIMIT = 56 * 1024 * 1024
NEG_INF = float("-inf")
LOG2E = math.log2(math.e)
KEY_NEG_INF = -2139095041

C_AQ, C_AK, C_AV, C_AG, C_IQ, C_IK, C_IW, C_HQ, C_HF, C_HI, C_HG, C_END = (
    0, 512, 768, 1024, 1536, 2048, 2176, 2304, 2816, 3328, 3840, 4352)
HEAD_PERM = (0, 2, 1, 3, 4, 6, 5, 7)


def _mxu(x):
    return x.astype(MXU_DTYPE)


def _dot(a, b):
    return jnp.dot(a, b, preferred_element_type=F32)


def _dot_nt(a, b):
    return lax.dot_general(a, b, (((1,), (1,)), ((), ())), preferred_element_type=F32)


def _dot_tn(a, b):
    return lax.dot_general(a, b, (((0,), (0,)), ((), ())), preferred_element_type=F32)


def _split_dot(x, w, parts):
    acc = None
    r = x
    for _ in range(parts):
        p = _mxu(r)
        t = _dot(p, w)
        acc = t if acc is None else acc + t
        r = r - p.astype(F32)
    return acc


def _sigmoid(x):
    return 1.0 / (1.0 + jnp.exp(-x))


def _silu(x):
    return x * _sigmoid(x)


def _t5_bucket(n):
    n = jnp.maximum(n, 0)
    max_exact = N_BUCKETS // 2
    nf = jnp.maximum(n, 1).astype(F32)
    large = max_exact + jnp.floor(jnp.log(nf / max_exact) / math.log(MAX_DISTANCE / max_exact)
                                  * (N_BUCKETS - max_exact)).astype(jnp.int32)
    large = jnp.minimum(large, N_BUCKETS - 1)
    return jnp.where(n < max_exact, n, large)


def _block_diag(n, blk, val, dtype):
    r = lax.broadcasted_iota(jnp.int32, (n, n), 0) // blk
    c = lax.broadcasted_iota(jnp.int32, (n, n), 1) // blk
    return jnp.where(r == c, val, 0.0).astype(dtype)


def _proj_kernel(x_ref, lng_ref, w_ref, qg_ref, kg_ref, lb_ref, bd_ref,
                 qt_ref, iqt_ref, iwt_ref, kf_ref, kb_ref, vf_ref, vb_ref, ikw_ref, ik2_ref,
                 sga_ref, hq_ref, hg_ref, hk_ref, hv_ref, sgh_ref):
    x = x_ref[...]
    ms = jnp.mean(x * x, axis=-1, keepdims=True)
    xb = _mxu(x * lax.rsqrt(ms + EPS) * lng_ref[...])

    def seg(a, b):
        return _dot(xb, w_ref[:, a:b])

    bd = bd_ref[...]

    aq = seg(C_AQ, C_AK)
    msq = _split_dot(aq * aq, bd, 2)
    qt_ref[...] = (aq * lax.rsqrt(msq + EPS) * qg_ref[...]
                   * (HEAD_DIM ** -0.5 * LOG2E)).T.astype(qt_ref.dtype)

    ak = seg(C_AK, C_AV)
    msk = _split_dot(ak * ak, bd[:D_KV, :D_KV], 2)
    k = ak * lax.rsqrt(msk + EPS) * kg_ref[...]
    kf_ref[...] = k
    kb_ref[...] = k.astype(kb_ref.dtype)

    v = seg(C_AV, C_AG)
    vf_ref[...] = v
    vb_ref[...] = v.astype(vb_ref.dtype)

    sga_ref[...] = _silu(seg(C_AG, C_IQ))
    iqt_ref[...] = seg(C_IQ, C_IK).T.astype(iqt_ref.dtype)

    ikk = seg(C_IK, C_IW)
    iww = seg(C_IW, C_HQ) * (IDX_HEADS ** -0.5 * IDX_DIM ** -0.5)
    ik2_ref[...] = ikk.astype(ik2_ref.dtype)
    iwt_ref[...] = iww.T[IDX_DIM:IDX_DIM + IDX_HEADS]
    lane = lax.broadcasted_iota(jnp.int32, ikk.shape, 1)
    ikw_ref[...] = jnp.where(lane < IDX_DIM, ikk, iww)

    def put_pairs(ref, val):
        for p in range(D_H // LANES):
            ref[p] = val[:, p * LANES:(p + 1) * LANES]

    put_pairs(hq_ref, _silu(seg(C_HQ, C_HF)) * H_KEY ** -0.5)
    lbp = lb_ref[...]
    mx = jnp.max(lbp, axis=0, keepdims=True)
    e = jnp.exp(lbp - mx)
    lb = e[0:1] / jnp.sum(e, axis=0, keepdims=True)
    f = lb + (1.0 - lb) * _sigmoid(seg(C_HF, C_HI))
    put_pairs(hg_ref, jnp.log(f))
    put_pairs(hk_ref, 1.0 - f)
    put_pairs(hv_ref, seg(C_HI, C_HG))
    sgh_ref[...] = _silu(seg(C_HG, C_END))


def _project(x2d, ln_g, w_pack, qg, kg, lbp, bd, tm):
    n, d = x2d.shape
    row = lambda w: pl.BlockSpec((tm, w), lambda i: (i, 0))
    full = lambda a: pl.BlockSpec(a.shape, lambda i: (0,) * a.ndim)
    outs = [("qt", -D_ATTN, MXU_DTYPE), ("iqt", -D_ATTN, MXU_DTYPE), ("iwt", -IDX_HEADS, F32),
            ("kf", D_KV, F32),
            ("kb", D_KV, MXU_DTYPE), ("vf", D_KV, F32), ("vb", D_KV, MXU_DTYPE),
            ("ikw", LANES, F32), ("ik2", LANES, MXU_DTYPE), ("sga", D_ATTN, F32),
            ("hq", 0, F32), ("hg", 0, F32), ("hk", 0, F32), ("hv", 0, F32),
            ("sgh", D_H, F32)]
    n_pairs = D_H // LANES

    def spec(w):
        if w > 0:
            return row(w), (n, w)
        if w < 0:
            return pl.BlockSpec((-w, tm), lambda i: (0, i)), (-w, n)
        return pl.BlockSpec((n_pairs, tm, LANES), lambda i: (0, i, 0)), (n_pairs, n, LANES)

    res = pl.pallas_call(
        _proj_kernel,
        grid=(n // tm,),
        in_specs=[row(d), full(ln_g), full(w_pack), full(qg), full(kg), full(lbp), full(bd)],
        out_specs=[spec(w)[0] for _, w, _ in outs],
        out_shape=[jax.ShapeDtypeStruct(spec(w)[1], dt) for _, w, dt in outs],
        compiler_params=pltpu.CompilerParams(
            dimension_semantics=("arbitrary",), vmem_limit_bytes=VMEM_LIMIT),
        name="proj",
    )(x2d, ln_g, w_pack, qg, kg, lbp, bd)
    return dict(zip([o[0] for o in outs], res))


def _bias_tile_kernel(tab_ref, o_ref):
    j = lax.broadcasted_iota(jnp.int32, (2 * Q_BLK, Q_BLK), 0)
    r = lax.broadcasted_iota(jnp.int32, (2 * Q_BLK, Q_BLK), 1)
    bucket = _t5_bucket(Q_BLK + r - j)
    for h in range(A_HEADS):
        acc = jnp.zeros((2 * Q_BLK, Q_BLK), F32)
        for b in range(N_BUCKETS):
            acc = jnp.where(bucket == b, tab_ref[b, h] * LOG2E, acc)
        o_ref[h] = acc


def _bias_tiles(rel_bias):
    return pl.pallas_call(
        _bias_tile_kernel,
        in_specs=[pl.BlockSpec(memory_space=pltpu.SMEM)],
        out_shape=jax.ShapeDtypeStruct((A_HEADS, 2 * Q_BLK, Q_BLK), F32),
        name="bias_tiles",
    )(rel_bias)


def _key_to_f32(key):
    bits = key ^ ((key >> 31) & jnp.int32(0x7FFFFFFF))
    return lax.bitcast_convert_type(bits, F32)


def _topk_cut(sc_ref, nch, cut_ref, tie_ref, extra=None):
    ncol = sc_ref.shape[1]
    nrows = nch * K_CHUNK
    nbits = int(sc_ref.shape[0]).bit_length()
    groups = K_CHUNK // 8

    def count(pred, ref, extra_val, side=None):
        def body(c, acc):
            r0 = pl.multiple_of(c * K_CHUNK, K_CHUNK)
            v = ref[pl.ds(r0, K_CHUNK), :]
            if side is not None:
                side(v, r0)
            hit = jnp.where(pred(v), 1, 0).astype(jnp.int32)
            return acc + jnp.sum(hit.reshape(groups, 8, ncol), axis=0, dtype=jnp.int32)

        acc = lax.fori_loop(0, nch, body, jnp.zeros((8, ncol), jnp.int32))
        cnt = jnp.sum(acc, axis=0, keepdims=True, dtype=jnp.int32)
        if extra_val is not None:
            cnt = cnt + jnp.where(pred(extra_val), 1, 0).astype(jnp.int32)
        return cnt

    def bit_body(it, carry):
        key, n_ge = carry
        cand = key + jnp.left_shift(jnp.int32(1), 31 - it)
        cf = _key_to_f32(cand)
        cnt = count(lambda v: v >= cf, sc_ref, extra)
        ok = cnt >= TOPK
        return jnp.where(ok, cand, key), jnp.where(ok, cnt, n_ge)

    key0 = jnp.full((1, ncol), jnp.iinfo(jnp.int32).min, jnp.int32)
    key, n_ge = lax.fori_loop(0, 32, bit_body, (key0, jnp.full((1, ncol), TOPK, jnp.int32)))
    below = key < KEY_NEG_INF
    thr = _key_to_f32(jnp.maximum(key, KEY_NEG_INF))
    cut_ref[...] = jnp.full((1, ncol), jnp.iinfo(jnp.int32).max, jnp.int32)

    @pl.when(jnp.max(jnp.where(below, TOPK + 1, n_ge)) > TOPK)
    def _():
        big = jnp.int32(1 << 30)

        def mark(v, r0):
            row = r0 + lax.broadcasted_iota(jnp.int32, v.shape, 0)
            tie_ref[pl.ds(r0, K_CHUNK), :] = jnp.where(v == thr, row, big)

        need = TOPK - count(lambda v: v > thr, sc_ref, extra, side=mark)
        extra_tie = None if extra is None else jnp.where(extra == thr, nrows, big)

        def idx_body(it, x):
            cand = x + jnp.left_shift(jnp.int32(1), nbits - 1 - it)
            g = count(lambda t: t < cand, tie_ref, extra_tie)
            return jnp.where(g < need, cand, x)

        cut_ref[...] = lax.fori_loop(0, nbits, idx_body, jnp.zeros((1, ncol), jnp.int32))

    return thr


def _in_topk(sc, col, thr, cut):
    return (sc > thr) | ((sc == thr) & (col <= cut))


def _row_halves(x):
    half = x.shape[0] // 2
    zero = jnp.zeros((half, x.shape[1]), x.dtype)
    return (jnp.concatenate([x[:half], zero], axis=0), jnp.concatenate([zero, x[half:]], axis=0))


def _attn_prompt_kernel(tab_ref, qt_ref, iqt_ref, iwt_ref, kb_ref, vt_ref, ik2_ref, bt_ref,
                        o_ref, sc_ref, tie_ref, cut_ref, m_ref, acc_ref):
    i = pl.program_id(1)
    t_pos = i * Q_BLK + lax.broadcasted_iota(jnp.int32, (1, Q_BLK), 1)

    @pl.when((pl.program_id(0) == 0) & (i == 0))
    def _():
        sc_ref[...] = jnp.full(sc_ref.shape, NEG_INF, F32)

    iqt = iqt_ref[...]
    cols = []
    for g in range(D_ATTN // LANES):
        cols.extend(_row_halves(iqt[g * LANES:(g + 1) * LANES]))
    iq_stack = jnp.concatenate(cols, axis=1)
    iw = iwt_ref[...]
    nch = ((i + 2) * Q_BLK + K_CHUNK - 1) // K_CHUNK

    def score_body(c, carry):
        r0 = pl.multiple_of(c * K_CHUNK, K_CHUNK)
        s_all = _dot(ik2_ref[0, pl.ds(r0, K_CHUNK), :], iq_stack)
        sc = jnp.zeros((K_CHUNK, Q_BLK), F32)
        for h in range(IDX_HEADS):
            sc = sc + jnp.maximum(s_all[:, h * Q_BLK:(h + 1) * Q_BLK], 0.0) * iw[h:h + 1]
        s_glob = r0 - Q_BLK + lax.broadcasted_iota(jnp.int32, (K_CHUNK, Q_BLK), 0)
        valid = (s_glob >= 0) & (s_glob <= t_pos)
        sc_ref[pl.ds(r0, K_CHUNK), :] = jnp.where(valid, sc, NEG_INF)
        return carry

    lax.fori_loop(0, nch, score_body, 0)
    thr = _topk_cut(sc_ref, nch, cut_ref, tie_ref)
    cut = cut_ref[...]

    m_ref[...] = jnp.full(m_ref.shape, NEG_INF, F32)
    acc_ref[...] = jnp.zeros(acc_ref.shape, F32)

    qt = qt_ref[...]
    q_pairs = []
    for p in range(2):
        g0 = _row_halves(qt[(2 * p) * LANES:(2 * p + 1) * LANES])
        g1 = _row_halves(qt[(2 * p + 1) * LANES:(2 * p + 2) * LANES])
        q_pairs.append(jnp.concatenate([g0[0], g1[0], g0[1], g1[1]], axis=1))

    def attend(r0, width, valid, bias_fn, const_fn):
        madd = jnp.where(valid, 0.0, NEG_INF)
        for p in range(2):
            kc = kb_ref[0, pl.ds(r0, width), p * LANES:(p + 1) * LANES]
            vc = vt_ref[0, p * V_ROWS:(p + 1) * V_ROWS, pl.ds(r0, width)]
            st = _dot(kc, q_pairs[p])
            ps, alphas = [], []
            for a in range(4):
                h = 4 * p + a
                x = st[:, a * Q_BLK:(a + 1) * Q_BLK] + madd
                bias = bias_fn(h)
                if bias is not None:
                    x = x + bias
                c = const_fn(h)
                m_old = m_ref[h:h + 1]
                m_new = jnp.maximum(m_old, jnp.max(x, axis=0, keepdims=True) + c)
                m_safe = jnp.where(m_new == NEG_INF, 0.0, m_new)
                m_ref[h:h + 1] = m_new
                ps.append(_mxu(jnp.exp2(x - (m_safe - c))))
                alphas.append(jnp.exp2(m_old - m_safe))
            pv = _dot(vc, jnp.concatenate(ps, axis=1))
            acc_ref[p] = acc_ref[p] * jnp.concatenate(alphas, axis=1) + pv

    def far_body(c, carry):
        r0 = pl.multiple_of(c * FAR_CHUNK, FAR_CHUNK)
        row = r0 + lax.broadcasted_iota(jnp.int32, (FAR_CHUNK, Q_BLK), 0)
        valid = ((row >= Q_BLK) & (row < i * Q_BLK)
                 & _in_topk(sc_ref[pl.ds(r0, FAR_CHUNK), :], row, thr, cut))
        attend(r0, FAR_CHUNK, valid, lambda h: None, lambda h: tab_ref[N_BUCKETS - 1, h] * LOG2E)
        return carry

    lax.fori_loop(0, (i * Q_BLK + FAR_CHUNK - 1) // FAR_CHUNK, far_body, 0)

    r0 = pl.multiple_of(i * Q_BLK, Q_BLK)
    row = r0 + lax.broadcasted_iota(jnp.int32, (2 * Q_BLK, Q_BLK), 0)
    valid = ((row >= Q_BLK) & (row - Q_BLK <= t_pos)
             & _in_topk(sc_ref[pl.ds(r0, 2 * Q_BLK), :], row, thr, cut))
    attend(r0, 2 * Q_BLK, valid, lambda h: bt_ref[h], lambda h: 0.0)

    top = lax.broadcasted_iota(jnp.int32, (LANES, Q_BLK), 0) < HEAD_DIM
    for p in range(2):
        acc = acc_ref[p]
        outs = [acc[:LANES, a * Q_BLK:(a + 1) * Q_BLK] / acc[LANES:LANES + 1, a * Q_BLK:(a + 1) * Q_BLK]
                for a in range(4)]
        o_ref[0, :, (2 * p) * LANES:(2 * p + 1) * LANES] = jnp.where(top, outs[0], outs[2]).T
        o_ref[0, :, (2 * p + 1) * LANES:(2 * p + 2) * LANES] = jnp.where(top, outs[1], outs[3]).T


def _attn_prompt(rel_bias, bias_tiles, qt, iqt, iwt, kbp, vtp, ik2p, b, s):
    nq = s // Q_BLK
    lp = kbp.shape[1]
    blk_t = lambda r: pl.BlockSpec((r, Q_BLK), lambda bi, i: (0, bi * nq + i))
    return pl.pallas_call(
        _attn_prompt_kernel,
        grid=(b, s // Q_BLK),
        in_specs=[pl.BlockSpec(memory_space=pltpu.SMEM),
                  blk_t(D_ATTN), blk_t(D_ATTN), blk_t(IDX_HEADS),
                  pl.BlockSpec((1, lp, D_KV), lambda bi, i: (bi, 0, 0)),
                  pl.BlockSpec((1, 2 * V_ROWS, lp), lambda bi, i: (bi, 0, 0)),
                  pl.BlockSpec((1, lp, LANES), lambda bi, i: (bi, 0, 0)),
                  pl.BlockSpec(bias_tiles.shape, lambda bi, i: (0, 0, 0))],
        out_specs=pl.BlockSpec((1, Q_BLK, D_ATTN), lambda bi, i: (bi, i, 0)),
        out_shape=jax.ShapeDtypeStruct((b, s, D_ATTN), F32),
        scratch_shapes=[pltpu.VMEM((lp, Q_BLK), F32),
                        pltpu.VMEM((lp, Q_BLK), jnp.int32),
                        pltpu.VMEM((1, Q_BLK), jnp.int32),
                        pltpu.VMEM((A_HEADS, Q_BLK), F32),
                        pltpu.VMEM((2, V_ROWS, 4 * Q_BLK), F32)],
        compiler_params=pltpu.CompilerParams(
            dimension_semantics=("arbitrary", "arbitrary"), vmem_limit_bytes=VMEM_LIMIT),
        name="attn_prompt",
    )(rel_bias, qt, iqt, iwt, kbp, vtp, ik2p, bias_tiles)


def _hgrn_prompt_kernel(q_ref, g_ref, k_ref, v_ref, o_ref, st_ref, b_ref, oi_ref,
                        tq_ref, tk_ref, tv_ref, tb_ref):
    j = pl.program_id(1)
    n_pairs = D_H // LANES
    n_chunks = HG_BLK // HG_CHUNK

    @pl.when(j == 0)
    def _():
        st_ref[...] = jnp.zeros(st_ref.shape, F32)

    def lanes(ref, rows):
        return jnp.concatenate([ref[p, rows, :] for p in range(n_pairs)], axis=1)

    r = lax.broadcasted_iota(jnp.int32, (HG_BLK, HG_BLK), 0)
    c = lax.broadcasted_iota(jnp.int32, (HG_BLK, HG_BLK), 1)
    tri = jnp.where(c <= r, 1.0, 0.0).astype(MXU_DTYPE)
    bcum = _split_dot_left(tri, lanes(g_ref, slice(None)), 3)
    for p in range(n_pairs):
        b_ref[p] = bcum[:, p * LANES:(p + 1) * LANES]

    for t in range(HG_CHUNK):
        rows = pl.ds(t, n_chunks, stride=HG_CHUNK)
        tq_ref[t] = lanes(q_ref, rows)
        tk_ref[t] = lanes(k_ref, rows)
        tv_ref[t] = lanes(v_ref, rows)
        tb_ref[t] = lanes(b_ref, rows)

    half = D_H // 2
    bd_half = _block_diag(half, H_KEY, 1.0, MXU_DTYPE)
    for t in range(HG_CHUNK):
        qt = tq_ref[t]
        bt = tb_ref[t]
        parts = [qt * tk_ref[s] * jnp.exp(bt - tb_ref[s]) for s in range(t)]
        parts.append(qt * tk_ref[t])
        pr = _mxu(jnp.concatenate(parts, axis=0))
        rs = jnp.concatenate([_dot(pr[:, :half], bd_half), _dot(pr[:, half:], bd_half)], axis=1)
        acc = rs[0:n_chunks] * tv_ref[0]
        for s in range(1, t + 1):
            acc = acc + rs[s * n_chunks:(s + 1) * n_chunks] * tv_ref[s]
        for p in range(n_pairs):
            oi_ref[p, pl.ds(t, n_chunks, stride=HG_CHUNK), :] = acc[:, p * LANES:(p + 1) * LANES]

    lane_r = lax.broadcasted_iota(jnp.int32, (LANES, LANES), 0) // H_KEY
    lane_c = lax.broadcasted_iota(jnp.int32, (LANES, LANES), 1) // H_KEY
    same_head = lane_r == lane_c
    for cch in range(n_chunks):
        rows = slice(cch * HG_CHUNK, (cch + 1) * HG_CHUNK)
        last = slice((cch + 1) * HG_CHUNK - 1, (cch + 1) * HG_CHUNK)
        prev = slice(cch * HG_CHUNK - 1, cch * HG_CHUNK)
        outs = []
        for p in range(n_pairs):
            bch = b_ref[p, rows, :]
            b0 = b_ref[p, prev, :] if cch > 0 else jnp.zeros((1, LANES), F32)
            bl = b_ref[p, last, :]
            qi = _mxu(q_ref[p, rows, :] * jnp.exp(bch - b0))
            kl = _mxu(k_ref[p, rows, :] * jnp.exp(bl - bch))
            st = st_ref[0, p]
            outs.append(oi_ref[p, rows, :] + _dot_nt(qi, _mxu(st)))
            ds = _dot_tn(_mxu(v_ref[p, rows, :]), kl)
            st_ref[0, p] = st * jnp.exp(bl - b0) + jnp.where(same_head, ds, 0.0)
        o_ref[0, rows, :] = jnp.concatenate(outs, axis=1)


def _split_dot_left(w, x, parts):
    acc = None
    r = x
    for _ in range(parts):
        p = _mxu(r)
        t = _dot(w, p)
        acc = t if acc is None else acc + t
        r = r - p.astype(F32)
    return acc


def _hgrn_prompt(hq, hg, hk, hv, b, s):
    n_pairs = D_H // LANES
    nblk = s // HG_BLK
    blk = pl.BlockSpec((n_pairs, HG_BLK, LANES), lambda bi, j: (0, bi * nblk + j, 0))
    return pl.pallas_call(
        _hgrn_prompt_kernel,
        grid=(b, nblk),
        in_specs=[blk, blk, blk, blk],
        out_specs=[pl.BlockSpec((1, HG_BLK, D_H), lambda bi, j: (bi, j, 0)),
                   pl.BlockSpec((1, n_pairs, LANES, LANES), lambda bi, j: (bi, 0, 0, 0))],
        out_shape=[jax.ShapeDtypeStruct((b, s, D_H), F32),
                   jax.ShapeDtypeStruct((b, n_pairs, LANES, LANES), F32)],
        scratch_shapes=[pltpu.VMEM((n_pairs, HG_BLK, LANES), F32)] * 2
        + [pltpu.VMEM((HG_CHUNK, HG_BLK // HG_CHUNK, D_H), F32)] * 4,
        compiler_params=pltpu.CompilerParams(
            dimension_semantics=("arbitrary", "arbitrary"), vmem_limit_bytes=VMEM_LIMIT),
        name="hgrn_prompt",
    )(hq, hg, hk, hv)


def _hgrn_sample_kernel(q_ref, g_ref, k_ref, v_ref, s_ref, o_ref, sn_ref):
    q = q_ref[...]
    e = jnp.exp(g_ref[...])
    k = k_ref[...]
    v = v_ref[...]
    o = jnp.zeros(v.shape, F32)
    for kk in range(H_KEY):
        sn = e[kk:kk + 1] * s_ref[0, kk] + k[kk:kk + 1] * v
        sn_ref[0, kk] = sn
        o = o + q[kk:kk + 1] * sn
    o_ref[...] = o


def _hgrn_sample(hq, hg, hk, hv, state_t):
    db = hq.shape[0]
    hspec = pl.BlockSpec((H_KEY, db), lambda h: (h, 0))
    sspec = pl.BlockSpec((1, H_KEY, H_KEY, db), lambda h: (h, 0, 0, 0))
    o_t, s_new = pl.pallas_call(
        _hgrn_sample_kernel,
        grid=(H_HEADS,),
        in_specs=[hspec, hspec, hspec, hspec, sspec],
        out_specs=[hspec, sspec],
        out_shape=[jax.ShapeDtypeStruct((D_H, db), F32),
                   jax.ShapeDtypeStruct(state_t.shape, F32)],
        compiler_params=pltpu.CompilerParams(
            dimension_semantics=("arbitrary",), vmem_limit_bytes=VMEM_LIMIT),
        name="hgrn_sample",
    )(hq.T, hg.T, hk.T, hv.T, state_t)
    return o_t.T, s_new


def _merge_kernel(x_ref, a_ref, sga_ref, h_ref, sgh_ref, hng_ref, bd_ref, w_ref, y_ref):
    h = h_ref[...]
    msq = _split_dot(h * h, bd_ref[...], 2)
    hn = h * lax.rsqrt(msq + EPS) * hng_ref[...]
    ma = _mxu(a_ref[...] * sga_ref[...])
    mh = _mxu(hn * sgh_ref[...])
    y_ref[...] = x_ref[...] + _dot(ma, w_ref[0:D_ATTN, :]) + _dot(mh, w_ref[D_ATTN:, :])


def _merge(x2d, a_out, sga, h_out, sgh, hng, bd, w_pack, tm):
    n, d = x2d.shape
    row = lambda w: pl.BlockSpec((tm, w), lambda i: (i, 0))
    full = lambda a: pl.BlockSpec(a.shape, lambda i: (0,) * a.ndim)
    return pl.pallas_call(
        _merge_kernel,
        grid=(n // tm,),
        in_specs=[row(d), row(D_ATTN), row(D_ATTN), row(D_H), row(D_H),
                  full(hng), full(bd), full(w_pack)],
        out_specs=row(d),
        out_shape=jax.ShapeDtypeStruct((n, d), F32),
        compiler_params=pltpu.CompilerParams(
            dimension_semantics=("arbitrary",), vmem_limit_bytes=VMEM_LIMIT),
        name="merge",
    )(x2d, a_out, sga, h_out, sgh, hng, bd, w_pack)


def _sample_score_kernel(pps, pt_ref, iqh_ref, iwc_ref, ikn_ref, *refs):
    ik_refs = refs[:pps]
    sc_ref, sn_ref = refs[pps:]
    iqh = iqh_ref[0]
    iwc = iwc_ref[0]
    ikt = jnp.concatenate([_mxu(r[0]) for r in ik_refs], axis=1)
    s = _dot(iqh, ikt)
    sc_ref[0] = jnp.sum(jnp.maximum(s, 0.0) * iwc, axis=0, keepdims=True)
    prod = iqh.astype(F32) * _mxu(ikn_ref[0]).astype(F32)
    s_new = jnp.sum(prod, axis=1, keepdims=True)
    s_new = jnp.sum(jnp.maximum(s_new, 0.0) * iwc, axis=0, keepdims=True)
    sn_ref[0] = jnp.broadcast_to(s_new, (1, LANES))


def _sample_scores(page_table, iqh, iwc, ikn, cache_ikt, pps):
    db, n_pages = page_table.shape
    page_spec = lambda r: pl.BlockSpec(
        (1, IDX_DIM, PAGE), lambda b, j, pt: (pt[b, j * pps + r], 0, 0))
    gs = pltpu.PrefetchScalarGridSpec(
        num_scalar_prefetch=1,
        grid=(db, n_pages // pps),
        in_specs=[pl.BlockSpec((1, IDX_HEADS, IDX_DIM), lambda b, j, pt: (b, 0, 0)),
                  pl.BlockSpec((1, IDX_HEADS, 1), lambda b, j, pt: (b, 0, 0)),
                  pl.BlockSpec((1, 1, IDX_DIM), lambda b, j, pt: (b, 0, 0))]
        + [page_spec(r) for r in range(pps)],
        out_specs=[pl.BlockSpec((1, 1, pps * PAGE), lambda b, j, pt: (b, 0, j)),
                   pl.BlockSpec((1, 1, LANES), lambda b, j, pt: (b, 0, 0))],
    )
    return pl.pallas_call(
        functools.partial(_sample_score_kernel, pps),
        grid_spec=gs,
        out_shape=[jax.ShapeDtypeStruct((db, 1, n_pages * PAGE), F32),
                   jax.ShapeDtypeStruct((db, 1, LANES), F32)],
        compiler_params=pltpu.CompilerParams(
            dimension_semantics=("arbitrary", "arbitrary"), vmem_limit_bytes=VMEM_LIMIT),
        name="sample_scores",
    )(page_table, iqh, iwc, ikn, *([cache_ikt] * pps))


def _sample_thr_kernel(sc_ref, sn_ref, thr_ref, cut_ref, tie_ref):
    past = sc_ref.shape[0]
    thr_ref[...] = _topk_cut(sc_ref, past // K_CHUNK, cut_ref, tie_ref, extra=sn_ref[...])


def _sample_threshold(scores_t, s_new_t):
    db = scores_t.shape[1]
    return pl.pallas_call(
        _sample_thr_kernel,
        out_shape=[jax.ShapeDtypeStruct((1, db), F32),
                   jax.ShapeDtypeStruct((1, db), jnp.int32)],
        scratch_shapes=[pltpu.VMEM(scores_t.shape, jnp.int32)],
        compiler_params=pltpu.CompilerParams(vmem_limit_bytes=VMEM_LIMIT),
        name="sample_threshold",
    )(scores_t, s_new_t)


def _sample_attn_kernel(pps, pt_ref, tb_ref, qs_ref, sc_ref, thr_ref, cut_ref, sn_ref,
                        kn_ref, vn_ref, *refs):
    k_refs = refs[:pps]
    v_refs = refs[pps:2 * pps]
    o_ref, m_ref, l_ref, acc_ref = refs[2 * pps:]
    j = pl.program_id(1)
    nj = pl.num_programs(1)
    n = pps * PAGE
    past = nj * n

    @pl.when(j == 0)
    def _():
        m_ref[...] = jnp.full(m_ref.shape, NEG_INF, F32)
        l_ref[...] = jnp.zeros(l_ref.shape, F32)
        acc_ref[...] = jnp.zeros(acc_ref.shape, F32)

    thr = thr_ref[0, :, 0:1]
    cut = cut_ref[0, :, 0:1]
    qs = qs_ref[0]
    lo = lax.broadcasted_iota(jnp.int32, (1, LANES), 1) < HEAD_DIM

    def update(x, pv_fn):
        m_old = m_ref[...]
        m_new = jnp.maximum(m_old, jnp.max(x, axis=1, keepdims=True))
        m_safe = jnp.where(m_new == NEG_INF, 0.0, m_new)
        alpha = jnp.exp2(m_old - m_safe)
        pe = jnp.exp2(x - m_safe)
        l_ref[...] = alpha * l_ref[...] + jnp.sum(pe, axis=1, keepdims=True)
        m_ref[...] = m_new
        acc_ref[...] = alpha * acc_ref[...] + pv_fn(_mxu(pe))

    def head_rows(b8):
        z = jnp.zeros((4, b8.shape[1]), F32)
        return jnp.concatenate([b8[0:4], z, b8[4:8], z], axis=0)

    s_idx = j * n + lax.broadcasted_iota(jnp.int32, (1, n), 1)
    s_last = s_idx[:, n - PAGE:]
    bucket = _t5_bucket(past - s_last)
    near = jnp.zeros((A_HEADS, PAGE), F32)
    for b in range(N_BUCKETS):
        near = jnp.where(bucket == b, tb_ref[b], near)
    far = tb_ref[N_BUCKETS - 1]
    bias8 = jnp.concatenate([far] * (pps - 1) + [jnp.where(j == nj - 1, near, far)], axis=1)
    maskadd = jnp.where(_in_topk(sc_ref[0], s_idx, thr, cut), 0.0, NEG_INF)
    kt = jnp.concatenate([_mxu(r[0]) for r in k_refs], axis=1)
    vt = jnp.concatenate([_mxu(r[0]) for r in v_refs], axis=1)
    update(_dot(qs, kt) + head_rows(bias8) + maskadd, lambda pe: _dot_nt(pe, vt))

    @pl.when(j == nj - 1)
    def _():
        valid = (_in_topk(sn_ref[0], past, thr, cut)
                 & (lax.broadcasted_iota(jnp.int32, (1, LANES), 1) == 0))
        madd = jnp.where(valid, 0.0, NEG_INF)
        kn = _mxu(jnp.broadcast_to(kn_ref[0], (LANES, D_KV)))
        vn = _mxu(jnp.broadcast_to(vn_ref[0], (LANES, D_KV)))
        update(_dot_nt(qs, kn) + head_rows(tb_ref[0]) + madd, lambda pe: _dot(pe, vn))
        out = acc_ref[...] / l_ref[...]
        for p in range(2):
            o = out[8 * p:8 * p + 4, p * LANES:(p + 1) * LANES]
            o_ref[0, :, (2 * p) * LANES:(2 * p + 1) * LANES] = jnp.where(lo, o[0:1], o[2:3])
            o_ref[0, :, (2 * p + 1) * LANES:(2 * p + 2) * LANES] = jnp.where(lo, o[1:2], o[3:4])


def _sample_attention(page_table, tab_b, qs, scores, thr, cut, s_new, k_new, v_new,
                      cache_kt, cache_vt, pps):
    db, n_pages = page_table.shape
    page_spec = lambda r: pl.BlockSpec(
        (1, D_KV, PAGE), lambda b, j, pt: (pt[b, j * pps + r], 0, 0))
    per_seq = lambda shape: pl.BlockSpec((1,) + shape, lambda b, j, pt: (b,) + (0,) * len(shape))
    gs = pltpu.PrefetchScalarGridSpec(
        num_scalar_prefetch=1,
        grid=(db, n_pages // pps),
        in_specs=[pl.BlockSpec(tab_b.shape, lambda b, j, pt: (0, 0, 0)),
                  per_seq((2 * A_HEADS, D_KV)),
                  pl.BlockSpec((1, 1, pps * PAGE), lambda b, j, pt: (b, 0, j)),
                  per_seq((1, LANES)), per_seq((1, LANES)), per_seq((1, LANES)),
                  per_seq((1, D_KV)), per_seq((1, D_KV))]
        + [page_spec(r) for r in range(pps)] * 2,
        out_specs=per_seq((1, D_ATTN)),
        scratch_shapes=[pltpu.VMEM((2 * A_HEADS, 1), F32),
                        pltpu.VMEM((2 * A_HEADS, 1), F32),
                        pltpu.VMEM((2 * A_HEADS, D_KV), F32)],
    )
    return pl.pallas_call(
        functools.partial(_sample_attn_kernel, pps),
        grid_spec=gs,
        out_shape=jax.ShapeDtypeStruct((db, 1, D_ATTN), F32),
        compiler_params=pltpu.CompilerParams(
            dimension_semantics=("arbitrary", "arbitrary"), vmem_limit_bytes=VMEM_LIMIT),
        name="sample_attn",
    )(page_table, tab_b, qs, scores, thr, cut, s_new, k_new, v_new,
      *([cache_kt] * pps), *([cache_vt] * pps))


def _permute_heads(a, axis):
    shape = a.shape
    a = a.reshape(shape[:axis] + (A_HEADS, HEAD_DIM) + shape[axis + 1:])
    a = jnp.take(a, jnp.array(HEAD_PERM), axis=axis)
    return a.reshape(shape)


def _pack_w_in(w):
    d = w.shape[0]
    o = [0]
    for wd in (D_ATTN, D_KV, D_KV, D_ATTN, IDX_HEADS * IDX_DIM, IDX_HEADS, IDX_DIM,
               D_H, D_H, D_H, D_H):
        o.append(o[-1] + wd)
    a_q, a_k, a_v, a_g, i_q, i_w, i_k, h_q, h_f, h_i, h_g = (
        w[:, o[n]:o[n + 1]] for n in range(11))
    z = lambda n: jnp.zeros((d, n), w.dtype)
    packed = jnp.concatenate(
        [_permute_heads(a_q, 1), a_k, a_v, _permute_heads(a_g, 1), i_q, i_k, i_k,
         z(IDX_DIM), i_w, z(LANES - IDX_DIM - IDX_HEADS), h_q, h_f, h_i, h_g], axis=1)
    assert packed.shape[1] == C_END
    return packed.astype(MXU_DTYPE)


def kernel(x_prompt, x_sample, cache_k, cache_v, cache_ik, state_hgrn, page_table, rel_bias,
           ln_g, w_in, q_norm_g, k_norm_g, hgrn_lb, hgrn_norm_g, w_out):
    b, s, d = x_prompt.shape
    db, dt, _ = x_sample.shape
    depth, n_pool = cache_k.shape[:2]
    n_pages = page_table.shape[1]
    past = n_pages * PAGE
    assert depth == 1 and dt == 1 and hgrn_lb.shape[0] == 2
    assert s % HG_BLK == 0 and s % FAR_CHUNK == 0 and s >= 4 * TOPK and past >= 4 * TOPK
    pps = min(MAX_PAGES_PER_STEP, n_pages)
    assert n_pages % pps == 0 and past % K_CHUNK == 0 and db % 8 == 0
    assert N_BUCKETS // 2 + int(math.log((PAGE + 1) / (N_BUCKETS // 2))
                                / math.log(MAX_DISTANCE / (N_BUCKETS // 2))
                                * (N_BUCKETS - N_BUCKETS // 2)) >= N_BUCKETS - 1

    w_pack = _pack_w_in(w_in[0])
    w_out_pack = jnp.concatenate(
        [_permute_heads(w_out[0][:D_ATTN], 0), w_out[0][D_ATTN:]], axis=0).astype(MXU_DTYPE)
    qg = jnp.tile(q_norm_g[0], A_HEADS)[None]
    kg = jnp.tile(k_norm_g[0], A_KV_HEADS)[None]
    hng = jnp.tile(hgrn_norm_g[0], H_HEADS)[None]
    bd = _block_diag(D_ATTN, HEAD_DIM, 1.0 / HEAD_DIM, MXU_DTYPE)
    lng = ln_g[0][None]

    pp = _project(x_prompt.reshape(b * s, d), lng, w_pack, qg, kg, hgrn_lb, bd, 256)
    r3 = lambda a: a.reshape(b, s, a.shape[-1])
    lp = -(-(s + Q_BLK) // K_CHUNK) * K_CHUNK
    padk = lambda a: jnp.pad(r3(a), ((0, 0), (Q_BLK, lp - s - Q_BLK), (0, 0)))
    vt = padk(pp["vb"]).transpose(0, 2, 1)
    ones = jnp.ones((b, V_ROWS - LANES, lp), vt.dtype)
    vtp = jnp.concatenate([vt[:, :LANES], ones, vt[:, LANES:], ones], axis=1)
    a_out = _attn_prompt(rel_bias, _bias_tiles(rel_bias), pp["qt"], pp["iqt"], pp["iwt"],
                         padk(pp["kb"]), vtp, padk(pp["ik2"]), b, s)
    h_out, st = _hgrn_prompt(pp["hq"], pp["hg"], pp["hk"], pp["hv"], b, s)
    y_prompt = _merge(x_prompt.reshape(b * s, d), a_out.reshape(b * s, D_ATTN), pp["sga"],
                      h_out.reshape(b * s, D_H), pp["sgh"], hng, bd, w_out_pack, 256)
    st = st.reshape(b, D_H // LANES, 2, H_KEY, 2, H_KEY)
    s_prompt = jnp.stack([st[:, :, e, :, e, :] for e in range(2)], axis=2)
    s_prompt = s_prompt.reshape(b, H_HEADS, H_KEY, H_KEY).transpose(0, 1, 3, 2)

    sp = _project(x_sample.reshape(db, d), lng, w_pack, qg, kg, hgrn_lb, bd, db)
    ik_s = sp["ikw"][:, :IDX_DIM]
    iw_s = sp["ikw"][:, IDX_DIM:IDX_DIM + IDX_HEADS]
    scores, s_new = _sample_scores(
        page_table, sp["iqt"].T.reshape(db, IDX_HEADS, IDX_DIM), iw_s.reshape(db, IDX_HEADS, 1),
        ik_s.reshape(db, 1, IDX_DIM), cache_ik[0].transpose(0, 2, 1),
        min(2 * MAX_PAGES_PER_STEP, n_pages))
    thr, cut = _sample_threshold(scores.reshape(db, past).T, s_new[:, :, 0].T)
    lane_b = lambda a: jnp.broadcast_to(a.reshape(db, 1, 1), (db, 1, LANES))
    q_s = sp["qt"].T
    qg4 = q_s.reshape(db, 2, 2, 2, HEAD_DIM)
    rows = []
    for p in range(2):
        g0lo, g0hi = qg4[:, p, 0, 0], qg4[:, p, 0, 1]
        g1lo, g1hi = qg4[:, p, 1, 0], qg4[:, p, 1, 1]
        z1 = jnp.zeros_like(g0lo)
        pr = jnp.stack([jnp.concatenate([g0lo, z1], -1), jnp.concatenate([g1lo, z1], -1),
                        jnp.concatenate([z1, g0hi], -1), jnp.concatenate([z1, g1hi], -1)]
                       + [jnp.zeros((db, LANES), q_s.dtype)] * 4, axis=1)
        rows.append(pr)
    zq = jnp.zeros_like(rows[0])
    qs = jnp.concatenate([jnp.concatenate([rows[0], zq], axis=-1),
                          jnp.concatenate([zq, rows[1]], axis=-1)], axis=1)
    tab_b = jnp.broadcast_to(rel_bias[:, :, None] * LOG2E, (N_BUCKETS, A_HEADS, LANES))
    page_t = lambda c: c[0].transpose(0, 2, 3, 1).reshape(n_pool, D_KV, PAGE)
    a_out_s = _sample_attention(
        page_table, tab_b, qs, scores, lane_b(thr), lane_b(cut), s_new,
        sp["kf"].reshape(db, 1, D_KV), sp["vf"].reshape(db, 1, D_KV),
        page_t(cache_k), page_t(cache_v), pps)
    unpair = lambda a: a.transpose(1, 0, 2).reshape(db, D_H)
    h_out_s, s_sample = _hgrn_sample(unpair(sp["hq"]), unpair(sp["hg"]), unpair(sp["hk"]),
                                     unpair(sp["hv"]),
                                     state_hgrn[0].transpose(1, 2, 3, 0))
    s_sample = s_sample.transpose(3, 0, 1, 2)
    y_sample = _merge(x_sample.reshape(db, d), a_out_s.reshape(db, D_ATTN), sp["sga"],
                      h_out_s.reshape(db, D_H), sp["sgh"], hng, bd, w_out_pack, db)

    kv5 = lambda a, n: a.reshape(1, n, -1, A_KV_HEADS, HEAD_DIM)
    return (y_prompt.reshape(b, s, d), y_sample.reshape(db, 1, d),
            kv5(pp["kf"], b), kv5(pp["vf"], b),
            pp["ikw"][:, :IDX_DIM].reshape(1, b, s, IDX_DIM), s_prompt[None],
            kv5(sp["kf"], db), kv5(sp["vf"], db),
            ik_s.reshape(1, db, 1, IDX_DIM), s_sample[None])
```

```python
import functools
import math

import jax
import jax.numpy as jnp
from jax import lax
from jax.experimental import pallas as pl
from jax.experimental.pallas import tpu as pltpu

F32 = jnp.float32
BF16 = jnp.bfloat16
MXU_DTYPE = BF16

HEAD_DIM = 64
A_HEADS = 8
A_KV_HEADS = 4
D_ATTN = A_HEADS * HEAD_DIM
D_KV = A_KV_HEADS * HEAD_DIM
IDX_HEADS = 8
IDX_DIM = 64
H_HEADS = 8
H_KEY = 64
D_H = H_HEADS * H_KEY
TOPK = 256
PAGE = 128
N_BUCKETS = 32
MAX_DISTANCE = 128
EPS = 1e-6
LANES = 128
Q_BLK = 256
K_CHUNK = 512
FAR_CHUNK = 1024
HG_BLK = 256
HG_CHUNK = 16
V_ROWS = LANES + 16
MAX_PAGES_PER_STEP = 32
VMEM_LIMIT = 56 * 1024 * 1024
NEG_INF = float("-inf")
LOG2E = math.log2(math.e)
KEY_NEG_INF = -2139095041

C_AQ, C_AK, C_AV, C_AG, C_IQ, C_IK, C_IW, C_HQ, C_HF, C_HI, C_HG, C_END = (
    0, 512, 768, 1024, 1536, 2048, 2176, 2304, 2816, 3328, 3840, 4352)
HEAD_PERM = (0, 2, 1, 3, 4, 6, 5, 7)


def _mxu(x):
    return x.astype(MXU_DTYPE)


def _dot(a, b):
    return jnp.dot(a, b, preferred_element_type=F32)


def _dot_nt(a, b):
    return lax.dot_general(a, b, (((1,), (1,)), ((), ())), preferred_element_type=F32)


def _dot_tn(a, b):
    return lax.dot_general(a, b, (((0,), (0,)), ((), ())), preferred_element_type=F32)


def _split_dot(x, w, parts):
    acc = None
    r = x
    for _ in range(parts):
        p = _mxu(r)
        t = _dot(p, w)
        acc = t if acc is None else acc + t
        r = r - p.astype(F32)
    return acc


def _sigmoid(x):
    return 1.0 / (1.0 + jnp.exp(-x))


def _silu(x):
    return x * _sigmoid(x)


def _t5_bucket(n):
    n = jnp.maximum(n, 0)
    max_exact = N_BUCKETS // 2
    nf = jnp.maximum(n, 1).astype(F32)
    large = max_exact + jnp.floor(jnp.log(nf / max_exact) / math.log(MAX_DISTANCE / max_exact)
                                  * (N_BUCKETS - max_exact)).astype(jnp.int32)
    large = jnp.minimum(large, N_BUCKETS - 1)
    return jnp.where(n < max_exact, n, large)


def _block_diag(n, blk, val, dtype):
    r = lax.broadcasted_iota(jnp.int32, (n, n), 0) // blk
    c = lax.broadcasted_iota(jnp.int32, (n, n), 1) // blk
    return jnp.where(r == c, val, 0.0).astype(dtype)


def _proj_kernel(x_ref, lng_ref, w_ref, qg_ref, kg_ref, lb_ref, bd_ref,
                 qt_ref, iqt_ref, iwt_ref, kf_ref, kb_ref, vf_ref, vb_ref, ikw_ref, ik2_ref,
                 sga_ref, hq_ref, hg_ref, hk_ref, hv_ref, sgh_ref):
    x = x_ref[...]
    ms = jnp.mean(x * x, axis=-1, keepdims=True)
    xb = _mxu(x * lax.rsqrt(ms + EPS) * lng_ref[...])

    def seg(a, b):
        return _dot(xb, w_ref[:, a:b])

    bd = bd_ref[...]

    aq = seg(C_AQ, C_AK)
    msq = _split_dot(aq * aq, bd, 2)
    qt_ref[...] = (aq * lax.rsqrt(msq + EPS) * qg_ref[...]
                   * (HEAD_DIM ** -0.5 * LOG2E)).T.astype(qt_ref.dtype)

    ak = seg(C_AK, C_AV)
    msk = _split_dot(ak * ak, bd[:D_KV, :D_KV], 2)
    k = ak * lax.rsqrt(msk + EPS) * kg_ref[...]
    kf_ref[...] = k
    kb_ref[...] = k.astype(kb_ref.dtype)

    v = seg(C_AV, C_AG)
    vf_ref[...] = v
    vb_ref[...] = v.astype(vb_ref.dtype)

    sga_ref[...] = _silu(seg(C_AG, C_IQ))
    iqt_ref[...] = seg(C_IQ, C_IK).T.astype(iqt_ref.dtype)

    ikk = seg(C_IK, C_IW)
    iww = seg(C_IW, C_HQ) * (IDX_HEADS ** -0.5 * IDX_DIM ** -0.5)
    ik2_ref[...] = ikk.astype(ik2_ref.dtype)
    iwt_ref[...] = iww.T[IDX_DIM:IDX_DIM + IDX_HEADS]
    lane = lax.broadcasted_iota(jnp.int32, ikk.shape, 1)
    ikw_ref[...] = jnp.where(lane < IDX_DIM, ikk, iww)

    def put_pairs(ref, val):
        for p in range(D_H // LANES):
            ref[p] = val[:, p * LANES:(p + 1) * LANES]

    put_pairs(hq_ref, _silu(seg(C_HQ, C_HF)) * H_KEY ** -0.5)
    lbp = lb_ref[...]
    mx = jnp.max(lbp, axis=0, keepdims=True)
    e = jnp.exp(lbp - mx)
    lb = e[0:1] / jnp.sum(e, axis=0, keepdims=True)
    f = lb + (1.0 - lb) * _sigmoid(seg(C_HF, C_HI))
    put_pairs(hg_ref, jnp.log(f))
    put_pairs(hk_ref, 1.0 - f)
    put_pairs(hv_ref, seg(C_HI, C_HG))
    sgh_ref[...] = _silu(seg(C_HG, C_END))


def _project(x2d, ln_g, w_pack, qg, kg, lbp, bd, tm):
    n, d = x2d.shape
    row = lambda w: pl.BlockSpec((tm, w), lambda i: (i, 0))
    full = lambda a: pl.BlockSpec(a.shape, lambda i: (0,) * a.ndim)
    outs = [("qt", -D_ATTN, MXU_DTYPE), ("iqt", -D_ATTN, MXU_DTYPE), ("iwt", -IDX_HEADS, F32),
            ("kf", D_KV, F32),
            ("kb", D_KV, MXU_DTYPE), ("vf", D_KV, F32), ("vb", D_KV, MXU_DTYPE),
            ("ikw", LANES, F32), ("ik2", LANES, MXU_DTYPE), ("sga", D_ATTN, F32),
            ("hq", 0, F32), ("hg", 0, F32), ("hk", 0, F32), ("hv", 0, F32),
            ("sgh", D_H, F32)]
    n_pairs = D_H // LANES

    def spec(w):
        if w > 0:
            return row(w), (n, w)
        if w < 0:
            return pl.BlockSpec((-w, tm), lambda i: (0, i)), (-w, n)
        return pl.BlockSpec((n_pairs, tm, LANES), lambda i: (0, i, 0)), (n_pairs, n, LANES)

    res = pl.pallas_call(
        _proj_kernel,
        grid=(n // tm,),
        in_specs=[row(d), full(ln_g), full(w_pack), full(qg), full(kg), full(lbp), full(bd)],
        out_specs=[spec(w)[0] for _, w, _ in outs],
        out_shape=[jax.ShapeDtypeStruct(spec(w)[1], dt) for _, w, dt in outs],
        compiler_params=pltpu.CompilerParams(
            dimension_semantics=("arbitrary",), vmem_limit_bytes=VMEM_LIMIT),
        name="proj",
    )(x2d, ln_g, w_pack, qg, kg, lbp, bd)
    return dict(zip([o[0] for o in outs], res))


def _bias_tile_kernel(tab_ref, o_ref):
    j = lax.broadcasted_iota(jnp.int32, (2 * Q_BLK, Q_BLK), 0)
    r = lax.broadcasted_iota(jnp.int32, (2 * Q_BLK, Q_BLK), 1)
    bucket = _t5_bucket(Q_BLK + r - j)
    for h in range(A_HEADS):
        acc = jnp.zeros((2 * Q_BLK, Q_BLK), F32)
        for b in range(N_BUCKETS):
            acc = jnp.where(bucket == b, tab_ref[b, h] * LOG2E, acc)
        o_ref[h] = acc


def _bias_tiles(rel_bias):
    return pl.pallas_call(
        _bias_tile_kernel,
        in_specs=[pl.BlockSpec(memory_space=pltpu.SMEM)],
        out_shape=jax.ShapeDtypeStruct((A_HEADS, 2 * Q_BLK, Q_BLK), F32),
        name="bias_tiles",
    )(rel_bias)


def _key_to_f32(key):
    bits = key ^ ((key >> 31) & jnp.int32(0x7FFFFFFF))
    return lax.bitcast_convert_type(bits, F32)


def _topk_cut(sc_ref, nch, cut_ref, tie_ref, extra=None):
    ncol = sc_ref.shape[1]
    nrows = nch * K_CHUNK
    nbits = int(sc_ref.shape[0]).bit_length()
    groups = K_CHUNK // 8

    def count(pred, ref, extra_val, side=None):
        def body(c, acc):
            r0 = pl.multiple_of(c * K_CHUNK, K_CHUNK)
            v = ref[pl.ds(r0, K_CHUNK), :]
            if side is not None:
                side(v, r0)
            hit = jnp.where(pred(v), 1, 0).astype(jnp.int32).reshape(8, groups // 8, 8, ncol)
            parts = [jnp.sum(hit[g], axis=0, dtype=jnp.int32) for g in range(8)]
            return acc + (((parts[0] + parts[1]) + (parts[2] + parts[3]))
                          + ((parts[4] + parts[5]) + (parts[6] + parts[7])))

        acc = lax.fori_loop(0, nch, body, jnp.zeros((8, ncol), jnp.int32))
        cnt = jnp.sum(acc, axis=0, keepdims=True, dtype=jnp.int32)
        if extra_val is not None:
            cnt = cnt + jnp.where(pred(extra_val), 1, 0).astype(jnp.int32)
        return cnt

    def bit_body(it, carry):
        key, n_ge = carry
        cand = key + jnp.left_shift(jnp.int32(1), 31 - it)
        cf = _key_to_f32(cand)
        cnt = count(lambda v: v >= cf, sc_ref, extra)
        ok = cnt >= TOPK
        return jnp.where(ok, cand, key), jnp.where(ok, cnt, n_ge)

    key0 = jnp.full((1, ncol), jnp.iinfo(jnp.int32).min, jnp.int32)
    key, n_ge = lax.fori_loop(0, 32, bit_body, (key0, jnp.full((1, ncol), TOPK, jnp.int32)))
    below = key < KEY_NEG_INF
    thr = _key_to_f32(jnp.maximum(key, KEY_NEG_INF))
    cut_ref[...] = jnp.full((1, ncol), jnp.iinfo(jnp.int32).max, jnp.int32)

    @pl.when(jnp.max(jnp.where(below, TOPK + 1, n_ge)) > TOPK)
    def _():
        big = jnp.int32(1 << 30)

        def mark(v, r0):
            row = r0 + lax.broadcasted_iota(jnp.int32, v.shape, 0)
            tie_ref[pl.ds(r0, K_CHUNK), :] = jnp.where(v == thr, row, big)

        need = TOPK - count(lambda v: v > thr, sc_ref, extra, side=mark)
        extra_tie = None if extra is None else jnp.where(extra == thr, nrows, big)

        def idx_body(it, x):
            cand = x + jnp.left_shift(jnp.int32(1), nbits - 1 - it)
            g = count(lambda t: t < cand, tie_ref, extra_tie)
            return jnp.where(g < need, cand, x)

        cut_ref[...] = lax.fori_loop(0, nbits, idx_body, jnp.zeros((1, ncol), jnp.int32))

    return thr


def _in_topk(sc, col, thr, cut):
    return (sc > thr) | ((sc == thr) & (col <= cut))


def _row_halves(x):
    half = x.shape[0] // 2
    zero = jnp.zeros((half, x.shape[1]), x.dtype)
    return (jnp.concatenate([x[:half], zero], axis=0), jnp.concatenate([zero, x[half:]], axis=0))


def _attn_prompt_kernel(tab_ref, qt_ref, iqt_ref, iwt_ref, kb_ref, vt_ref, ik2_ref, bt_ref,
                        o_ref, sc_ref, tie_ref, cut_ref, m_ref, acc_ref):
    i = pl.program_id(1)
    t_pos = i * Q_BLK + lax.broadcasted_iota(jnp.int32, (1, Q_BLK), 1)

    @pl.when((pl.program_id(0) == 0) & (i == 0))
    def _():
        sc_ref[...] = jnp.full(sc_ref.shape, NEG_INF, F32)

    iqt = iqt_ref[...]
    cols = []
    for g in range(D_ATTN // LANES):
        cols.extend(_row_halves(iqt[g * LANES:(g + 1) * LANES]))
    iq_stack = jnp.concatenate(cols, axis=1)
    iw = iwt_ref[...]
    nch = ((i + 2) * Q_BLK + K_CHUNK - 1) // K_CHUNK

    def score_body(c, carry):
        r0 = pl.multiple_of(c * K_CHUNK, K_CHUNK)
        s_all = _dot(ik2_ref[0, pl.ds(r0, K_CHUNK), :], iq_stack)
        sc = jnp.zeros((K_CHUNK, Q_BLK), F32)
        for h in range(IDX_HEADS):
            sc = sc + jnp.maximum(s_all[:, h * Q_BLK:(h + 1) * Q_BLK], 0.0) * iw[h:h + 1]
        s_glob = r0 - Q_BLK + lax.broadcasted_iota(jnp.int32, (K_CHUNK, Q_BLK), 0)
        valid = (s_glob >= 0) & (s_glob <= t_pos)
        sc_ref[pl.ds(r0, K_CHUNK), :] = jnp.where(valid, sc, NEG_INF)
        return carry

    lax.fori_loop(0, nch, score_body, 0)
    thr = _topk_cut(sc_ref, nch, cut_ref, tie_ref)
    cut = cut_ref[...]

    m_ref[...] = jnp.full(m_ref.shape, NEG_INF, F32)
    acc_ref[...] = jnp.zeros(acc_ref.shape, F32)

    qt = qt_ref[...]
    q_pairs = []
    for p in range(2):
        g0 = _row_halves(qt[(2 * p) * LANES:(2 * p + 1) * LANES])
        g1 = _row_halves(qt[(2 * p + 1) * LANES:(2 * p + 2) * LANES])
        q_pairs.append(jnp.concatenate([g0[0], g1[0], g0[1], g1[1]], axis=1))

    def attend(r0, width, valid, bias_fn, const_fn):
        madd = jnp.where(valid, 0.0, NEG_INF)
        for p in range(2):
            kc = kb_ref[0, pl.ds(r0, width), p * LANES:(p + 1) * LANES]
            vc = vt_ref[0, p * V_ROWS:(p + 1) * V_ROWS, pl.ds(r0, width)]
            st = _dot(kc, q_pairs[p])
            ps, alphas = [], []
            for a in range(4):
                h = 4 * p + a
                x = st[:, a * Q_BLK:(a + 1) * Q_BLK] + madd
                bias = bias_fn(h)
                if bias is not None:
                    x = x + bias
                c = const_fn(h)
                m_old = m_ref[h:h + 1]
                m_new = jnp.maximum(m_old, jnp.max(x, axis=0, keepdims=True) + c)
                m_safe = jnp.where(m_new == NEG_INF, 0.0, m_new)
                m_ref[h:h + 1] = m_new
                ps.append(_mxu(jnp.exp2(x - (m_safe - c))))
                alphas.append(jnp.exp2(m_old - m_safe))
            pv = _dot(vc, jnp.concatenate(ps, axis=1))
            acc_ref[p] = acc_ref[p] * jnp.concatenate(alphas, axis=1) + pv

    def far_body(c, carry):
        r0 = pl.multiple_of(c * FAR_CHUNK, FAR_CHUNK)
        row = r0 + lax.broadcasted_iota(jnp.int32, (FAR_CHUNK, Q_BLK), 0)
        valid = ((row >= Q_BLK) & (row < i * Q_BLK)
                 & _in_topk(sc_ref[pl.ds(r0, FAR_CHUNK), :], row, thr, cut))
        attend(r0, FAR_CHUNK, valid, lambda h: None, lambda h: tab_ref[N_BUCKETS - 1, h] * LOG2E)
        return carry

    lax.fori_loop(0, (i * Q_BLK + FAR_CHUNK - 1) // FAR_CHUNK, far_body, 0)

    r0 = pl.multiple_of(i * Q_BLK, Q_BLK)
    row = r0 + lax.broadcasted_iota(jnp.int32, (2 * Q_BLK, Q_BLK), 0)
    valid = ((row >= Q_BLK) & (row - Q_BLK <= t_pos)
             & _in_topk(sc_ref[pl.ds(r0, 2 * Q_BLK), :], row, thr, cut))
    attend(r0, 2 * Q_BLK, valid, lambda h: bt_ref[h], lambda h: 0.0)

    top = lax.broadcasted_iota(jnp.int32, (LANES, Q_BLK), 0) < HEAD_DIM
    for p in range(2):
        acc = acc_ref[p]
        outs = [acc[:LANES, a * Q_BLK:(a + 1) * Q_BLK] / acc[LANES:LANES + 1, a * Q_BLK:(a + 1) * Q_BLK]
                for a in range(4)]
        o_ref[0, :, (2 * p) * LANES:(2 * p + 1) * LANES] = jnp.where(top, outs[0], outs[2]).T
        o_ref[0, :, (2 * p + 1) * LANES:(2 * p + 2) * LANES] = jnp.where(top, outs[1], outs[3]).T


def _attn_prompt(rel_bias, bias_tiles, qt, iqt, iwt, kbp, vtp, ik2p, b, s):
    nq = s // Q_BLK
    lp = kbp.shape[1]
    blk_t = lambda r: pl.BlockSpec((r, Q_BLK), lambda bi, i: (0, bi * nq + i))
    return pl.pallas_call(
        _attn_prompt_kernel,
        grid=(b, s // Q_BLK),
        in_specs=[pl.BlockSpec(memory_space=pltpu.SMEM),
                  blk_t(D_ATTN), blk_t(D_ATTN), blk_t(IDX_HEADS),
                  pl.BlockSpec((1, lp, D_KV), lambda bi, i: (bi, 0, 0)),
                  pl.BlockSpec((1, 2 * V_ROWS, lp), lambda bi, i: (bi, 0, 0)),
                  pl.BlockSpec((1, lp, LANES), lambda bi, i: (bi, 0, 0)),
                  pl.BlockSpec(bias_tiles.shape, lambda bi, i: (0, 0, 0))],
        out_specs=pl.BlockSpec((1, Q_BLK, D_ATTN), lambda bi, i: (bi, i, 0)),
        out_shape=jax.ShapeDtypeStruct((b, s, D_ATTN), F32),
        scratch_shapes=[pltpu.VMEM((lp, Q_BLK), F32),
                        pltpu.VMEM((lp, Q_BLK), jnp.int32),
                        pltpu.VMEM((1, Q_BLK), jnp.int32),
                        pltpu.VMEM((A_HEADS, Q_BLK), F32),
                        pltpu.VMEM((2, V_ROWS, 4 * Q_BLK), F32)],
        compiler_params=pltpu.CompilerParams(
            dimension_semantics=("arbitrary", "arbitrary"), vmem_limit_bytes=VMEM_LIMIT),
        name="attn_prompt",
    )(rel_bias, qt, iqt, iwt, kbp, vtp, ik2p, bias_tiles)


def _hgrn_prompt_kernel(q_ref, g_ref, k_ref, v_ref, o_ref, st_ref, b_ref, oi_ref,
                        tq_ref, tk_ref, tv_ref, tb_ref):
    j = pl.program_id(1)
    n_pairs = D_H // LANES
    n_chunks = HG_BLK // HG_CHUNK

    @pl.when(j == 0)
    def _():
        st_ref[...] = jnp.zeros(st_ref.shape, F32)

    def lanes(ref, rows):
        return jnp.concatenate([ref[p, rows, :] for p in range(n_pairs)], axis=1)

    r = lax.broadcasted_iota(jnp.int32, (HG_BLK, HG_BLK), 0)
    c = lax.broadcasted_iota(jnp.int32, (HG_BLK, HG_BLK), 1)
    tri = jnp.where(c <= r, 1.0, 0.0).astype(MXU_DTYPE)
    bcum = _split_dot_left(tri, lanes(g_ref, slice(None)), 3)
    for p in range(n_pairs):
        b_ref[p] = bcum[:, p * LANES:(p + 1) * LANES]

    for t in range(HG_CHUNK):
        rows = pl.ds(t, n_chunks, stride=HG_CHUNK)
        tq_ref[t] = lanes(q_ref, rows)
        tk_ref[t] = lanes(k_ref, rows)
        tv_ref[t] = lanes(v_ref, rows)
        tb_ref[t] = lanes(b_ref, rows)

    half = D_H // 2
    bd_half = _block_diag(half, H_KEY, 1.0, MXU_DTYPE)
    for t in range(HG_CHUNK):
        qt = tq_ref[t]
        bt = tb_ref[t]
        parts = [qt * tk_ref[s] * jnp.exp(bt - tb_ref[s]) for s in range(t)]
        parts.append(qt * tk_ref[t])
        pr = _mxu(jnp.concatenate(parts, axis=0))
        rs = jnp.concatenate([_dot(pr[:, :half], bd_half), _dot(pr[:, half:], bd_half)], axis=1)
        acc = rs[0:n_chunks] * tv_ref[0]
        for s in range(1, t + 1):
            acc = acc + rs[s * n_chunks:(s + 1) * n_chunks] * tv_ref[s]
        for p in range(n_pairs):
            oi_ref[p, pl.ds(t, n_chunks, stride=HG_CHUNK), :] = acc[:, p * LANES:(p + 1) * LANES]

    lane_r = lax.broadcasted_iota(jnp.int32, (LANES, LANES), 0) // H_KEY
    lane_c = lax.broadcasted_iota(jnp.int32, (LANES, LANES), 1) // H_KEY
    same_head = lane_r == lane_c
    for cch in range(n_chunks):
        rows = slice(cch * HG_CHUNK, (cch + 1) * HG_CHUNK)
        last = slice((cch + 1) * HG_CHUNK - 1, (cch + 1) * HG_CHUNK)
        prev = slice(cch * HG_CHUNK - 1, cch * HG_CHUNK)
        outs = []
        for p in range(n_pairs):
            bch = b_ref[p, rows, :]
            b0 = b_ref[p, prev, :] if cch > 0 else jnp.zeros((1, LANES), F32)
            bl = b_ref[p, last, :]
            qi = _mxu(q_ref[p, rows, :] * jnp.exp(bch - b0))
            kl = _mxu(k_ref[p, rows, :] * jnp.exp(bl - bch))
            st = st_ref[0, p]
            outs.append(oi_ref[p, rows, :] + _dot_nt(qi, _mxu(st)))
            ds = _dot_tn(_mxu(v_ref[p, rows, :]), kl)
            st_ref[0, p] = st * jnp.exp(bl - b0) + jnp.where(same_head, ds, 0.0)
        o_ref[0, rows, :] = jnp.concatenate(outs, axis=1)


def _split_dot_left(w, x, parts):
    acc = None
    r = x
    for _ in range(parts):
        p = _mxu(r)
        t = _dot(w, p)
        acc = t if acc is None else acc + t
        r = r - p.astype(F32)
    return acc


def _hgrn_prompt(hq, hg, hk, hv, b, s):
    n_pairs = D_H // LANES
    nblk = s // HG_BLK
    blk = pl.BlockSpec((n_pairs, HG_BLK, LANES), lambda bi, j: (0, bi * nblk + j, 0))
    return pl.pallas_call(
        _hgrn_prompt_kernel,
        grid=(b, nblk),
        in_specs=[blk, blk, blk, blk],
        out_specs=[pl.BlockSpec((1, HG_BLK, D_H), lambda bi, j: (bi, j, 0)),
                   pl.BlockSpec((1, n_pairs, LANES, LANES), lambda bi, j: (bi, 0, 0, 0))],
        out_shape=[jax.ShapeDtypeStruct((b, s, D_H), F32),
                   jax.ShapeDtypeStruct((b, n_pairs, LANES, LANES), F32)],
        scratch_shapes=[pltpu.VMEM((n_pairs, HG_BLK, LANES), F32)] * 2
        + [pltpu.VMEM((HG_CHUNK, HG_BLK // HG_CHUNK, D_H), F32)] * 4,
        compiler_params=pltpu.CompilerParams(
            dimension_semantics=("arbitrary", "arbitrary"), vmem_limit_bytes=VMEM_LIMIT),
        name="hgrn_prompt",
    )(hq, hg, hk, hv)


def _hgrn_sample_kernel(q_ref, g_ref, k_ref, v_ref, s_ref, o_ref, sn_ref):
    q = q_ref[...]
    e = jnp.exp(g_ref[...])
    k = k_ref[...]
    v = v_ref[...]
    o = jnp.zeros(v.shape, F32)
    for kk in range(H_KEY):
        sn = e[kk:kk + 1] * s_ref[0, kk] + k[kk:kk + 1] * v
        sn_ref[0, kk] = sn
        o = o + q[kk:kk + 1] * sn
    o_ref[...] = o


def _hgrn_sample(hq, hg, hk, hv, state_t):
    db = hq.shape[0]
    hspec = pl.BlockSpec((H_KEY, db), lambda h: (h, 0))
    sspec = pl.BlockSpec((1, H_KEY, H_KEY, db), lambda h: (h, 0, 0, 0))
    o_t, s_new = pl.pallas_call(
        _hgrn_sample_kernel,
        grid=(H_HEADS,),
        in_specs=[hspec, hspec, hspec, hspec, sspec],
        out_specs=[hspec, sspec],
        out_shape=[jax.ShapeDtypeStruct((D_H, db), F32),
                   jax.ShapeDtypeStruct(state_t.shape, F32)],
        compiler_params=pltpu.CompilerParams(
            dimension_semantics=("arbitrary",), vmem_limit_bytes=VMEM_LIMIT),
        name="hgrn_sample",
    )(hq.T, hg.T, hk.T, hv.T, state_t)
    return o_t.T, s_new


def _merge_kernel(x_ref, a_ref, sga_ref, h_ref, sgh_ref, hng_ref, bd_ref, w_ref, y_ref):
    h = h_ref[...]
    msq = _split_dot(h * h, bd_ref[...], 2)
    hn = h * lax.rsqrt(msq + EPS) * hng_ref[...]
    ma = _mxu(a_ref[...] * sga_ref[...])
    mh = _mxu(hn * sgh_ref[...])
    y_ref[...] = x_ref[...] + _dot(ma, w_ref[0:D_ATTN, :]) + _dot(mh, w_ref[D_ATTN:, :])


def _merge(x2d, a_out, sga, h_out, sgh, hng, bd, w_pack, tm):
    n, d = x2d.shape
    row = lambda w: pl.BlockSpec((tm, w), lambda i: (i, 0))
    full = lambda a: pl.BlockSpec(a.shape, lambda i: (0,) * a.ndim)
    return pl.pallas_call(
        _merge_kernel,
        grid=(n // tm,),
        in_specs=[row(d), row(D_ATTN), row(D_ATTN), row(D_H), row(D_H),
                  full(hng), full(bd), full(w_pack)],
        out_specs=row(d),
        out_shape=jax.ShapeDtypeStruct((n, d), F32),
        compiler_params=pltpu.CompilerParams(
            dimension_semantics=("arbitrary",), vmem_limit_bytes=VMEM_LIMIT),
        name="merge",
    )(x2d, a_out, sga, h_out, sgh, hng, bd, w_pack)


def _sample_score_kernel(pps, pt_ref, iqh_ref, iwc_ref, ikn_ref, *refs):
    ik_refs = refs[:pps]
    sc_ref, sn_ref = refs[pps:]
    iqh = iqh_ref[0]
    iwc = iwc_ref[0]
    ikt = jnp.concatenate([_mxu(r[0]) for r in ik_refs], axis=1)
    s = _dot(iqh, ikt)
    sc_ref[0] = jnp.sum(jnp.maximum(s, 0.0) * iwc, axis=0, keepdims=True)
    prod = iqh.astype(F32) * _mxu(ikn_ref[0]).astype(F32)
    s_new = jnp.sum(prod, axis=1, keepdims=True)
    s_new = jnp.sum(jnp.maximum(s_new, 0.0) * iwc, axis=0, keepdims=True)
    sn_ref[0] = jnp.broadcast_to(s_new, (1, LANES))


def _sample_scores(page_table, iqh, iwc, ikn, cache_ikt, pps):
    db, n_pages = page_table.shape
    page_spec = lambda r: pl.BlockSpec(
        (1, IDX_DIM, PAGE), lambda b, j, pt: (pt[b, j * pps + r], 0, 0))
    gs = pltpu.PrefetchScalarGridSpec(
        num_scalar_prefetch=1,
        grid=(db, n_pages // pps),
        in_specs=[pl.BlockSpec((1, IDX_HEADS, IDX_DIM), lambda b, j, pt: (b, 0, 0)),
                  pl.BlockSpec((1, IDX_HEADS, 1), lambda b, j, pt: (b, 0, 0)),
                  pl.BlockSpec((1, 1, IDX_DIM), lambda b, j, pt: (b, 0, 0))]
        + [page_spec(r) for r in range(pps)],
        out_specs=[pl.BlockSpec((1, 1, pps * PAGE), lambda b, j, pt: (b, 0, j)),
                   pl.BlockSpec((1, 1, LANES), lambda b, j, pt: (b, 0, 0))],
    )
    return pl.pallas_call(
        functools.partial(_sample_score_kernel, pps),
        grid_spec=gs,
        out_shape=[jax.ShapeDtypeStruct((db, 1, n_pages * PAGE), F32),
                   jax.ShapeDtypeStruct((db, 1, LANES), F32)],
        compiler_params=pltpu.CompilerParams(
            dimension_semantics=("arbitrary", "arbitrary"), vmem_limit_bytes=VMEM_LIMIT),
        name="sample_scores",
    )(page_table, iqh, iwc, ikn, *([cache_ikt] * pps))


def _sample_thr_kernel(sc_ref, sn_ref, thr_ref, cut_ref, tie_ref):
    past = sc_ref.shape[0]
    thr_ref[...] = _topk_cut(sc_ref, past // K_CHUNK, cut_ref, tie_ref, extra=sn_ref[...])


def _sample_threshold(scores_t, s_new_t):
    db = scores_t.shape[1]
    return pl.pallas_call(
        _sample_thr_kernel,
        out_shape=[jax.ShapeDtypeStruct((1, db), F32),
                   jax.ShapeDtypeStruct((1, db), jnp.int32)],
        scratch_shapes=[pltpu.VMEM(scores_t.shape, jnp.int32)],
        compiler_params=pltpu.CompilerParams(vmem_limit_bytes=VMEM_LIMIT),
        name="sample_threshold",
    )(scores_t, s_new_t)


def _sample_attn_kernel(pps, pt_ref, tb_ref, qs_ref, sc_ref, thr_ref, cut_ref, sn_ref,
                        kn_ref, vn_ref, *refs):
    k_refs = refs[:pps]
    v_refs = refs[pps:2 * pps]
    o_ref, m_ref, l_ref, acc_ref = refs[2 * pps:]
    j = pl.program_id(1)
    nj = pl.num_programs(1)
    n = pps * PAGE
    past = nj * n

    @pl.when(j == 0)
    def _():
        m_ref[...] = jnp.full(m_ref.shape, NEG_INF, F32)
        l_ref[...] = jnp.zeros(l_ref.shape, F32)
        acc_ref[...] = jnp.zeros(acc_ref.shape, F32)

    thr = thr_ref[0, :, 0:1]
    cut = cut_ref[0, :, 0:1]
    qs = qs_ref[0]
    lo = lax.broadcasted_iota(jnp.int32, (1, LANES), 1) < HEAD_DIM

    def update(x, pv_fn):
        m_old = m_ref[...]
        m_new = jnp.maximum(m_old, jnp.max(x, axis=1, keepdims=True))
        m_safe = jnp.where(m_new == NEG_INF, 0.0, m_new)
        alpha = jnp.exp2(m_old - m_safe)
        pe = jnp.exp2(x - m_safe)
        l_ref[...] = alpha * l_ref[...] + jnp.sum(pe, axis=1, keepdims=True)
        m_ref[...] = m_new
        acc_ref[...] = alpha * acc_ref[...] + pv_fn(_mxu(pe))

    def head_rows(b8):
        z = jnp.zeros((4, b8.shape[1]), F32)
        return jnp.concatenate([b8[0:4], z, b8[4:8], z], axis=0)

    s_idx = j * n + lax.broadcasted_iota(jnp.int32, (1, n), 1)
    s_last = s_idx[:, n - PAGE:]
    bucket = _t5_bucket(past - s_last)
    near = jnp.zeros((A_HEADS, PAGE), F32)
    for b in range(N_BUCKETS):
        near = jnp.where(bucket == b, tb_ref[b], near)
    far = tb_ref[N_BUCKETS - 1]
    bias8 = jnp.concatenate([far] * (pps - 1) + [jnp.where(j == nj - 1, near, far)], axis=1)
    maskadd = jnp.where(_in_topk(sc_ref[0], s_idx, thr, cut), 0.0, NEG_INF)
    kt = jnp.concatenate([_mxu(r[0]) for r in k_refs], axis=1)
    vt = jnp.concatenate([_mxu(r[0]) for r in v_refs], axis=1)
    update(_dot(qs, kt) + head_rows(bias8) + maskadd, lambda pe: _dot_nt(pe, vt))

    @pl.when(j == nj - 1)
    def _():
        valid = (_in_topk(sn_ref[0], past, thr, cut)
                 & (lax.broadcasted_iota(jnp.int32, (1, LANES), 1) == 0))
        madd = jnp.where(valid, 0.0, NEG_INF)
        kn = _mxu(jnp.broadcast_to(kn_ref[0], (LANES, D_KV)))
        vn = _mxu(jnp.broadcast_to(vn_ref[0], (LANES, D_KV)))
        update(_dot_nt(qs, kn) + head_rows(tb_ref[0]) + madd, lambda pe: _dot(pe, vn))
        out = acc_ref[...] / l_ref[...]
        for p in range(2):
            o = out[8 * p:8 * p + 4, p * LANES:(p + 1) * LANES]
            o_ref[0, :, (2 * p) * LANES:(2 * p + 1) * LANES] = jnp.where(lo, o[0:1], o[2:3])
            o_ref[0, :, (2 * p + 1) * LANES:(2 * p + 2) * LANES] = jnp.where(lo, o[1:2], o[3:4])


def _sample_attention(page_table, tab_b, qs, scores, thr, cut, s_new, k_new, v_new,
                      cache_kt, cache_vt, pps):
    db, n_pages = page_table.shape
    page_spec = lambda r: pl.BlockSpec(
        (1, D_KV, PAGE), lambda b, j, pt: (pt[b, j * pps + r], 0, 0))
    per_seq = lambda shape: pl.BlockSpec((1,) + shape, lambda b, j, pt: (b,) + (0,) * len(shape))
    gs = pltpu.PrefetchScalarGridSpec(
        num_scalar_prefetch=1,
        grid=(db, n_pages // pps),
        in_specs=[pl.BlockSpec(tab_b.shape, lambda b, j, pt: (0, 0, 0)),
                  per_seq((2 * A_HEADS, D_KV)),
                  pl.BlockSpec((1, 1, pps * PAGE), lambda b, j, pt: (b, 0, j)),
                  per_seq((1, LANES)), per_seq((1, LANES)), per_seq((1, LANES)),
                  per_seq((1, D_KV)), per_seq((1, D_KV))]
        + [page_spec(r) for r in range(pps)] * 2,
        out_specs=per_seq((1, D_ATTN)),
        scratch_shapes=[pltpu.VMEM((2 * A_HEADS, 1), F32),
                        pltpu.VMEM((2 * A_HEADS, 1), F32),
                        pltpu.VMEM((2 * A_HEADS, D_KV), F32)],
    )
    return pl.pallas_call(
        functools.partial(_sample_attn_kernel, pps),
        grid_spec=gs,
        out_shape=jax.ShapeDtypeStruct((db, 1, D_ATTN), F32),
        compiler_params=pltpu.CompilerParams(
            dimension_semantics=("arbitrary", "arbitrary"), vmem_limit_bytes=VMEM_LIMIT),
        name="sample_attn",
    )(page_table, tab_b, qs, scores, thr, cut, s_new, k_new, v_new,
      *([cache_kt] * pps), *([cache_vt] * pps))


def _permute_heads(a, axis):
    shape = a.shape
    a = a.reshape(shape[:axis] + (A_HEADS, HEAD_DIM) + shape[axis + 1:])
    a = jnp.take(a, jnp.array(HEAD_PERM), axis=axis)
    return a.reshape(shape)


def _pack_w_in(w):
    d = w.shape[0]
    o = [0]
    for wd in (D_ATTN, D_KV, D_KV, D_ATTN, IDX_HEADS * IDX_DIM, IDX_HEADS, IDX_DIM,
               D_H, D_H, D_H, D_H):
        o.append(o[-1] + wd)
    a_q, a_k, a_v, a_g, i_q, i_w, i_k, h_q, h_f, h_i, h_g = (
        w[:, o[n]:o[n + 1]] for n in range(11))
    z = lambda n: jnp.zeros((d, n), w.dtype)
    packed = jnp.concatenate(
        [_permute_heads(a_q, 1), a_k, a_v, _permute_heads(a_g, 1), i_q, i_k, i_k,
         z(IDX_DIM), i_w, z(LANES - IDX_DIM - IDX_HEADS), h_q, h_f, h_i, h_g], axis=1)
    assert packed.shape[1] == C_END
    return packed.astype(MXU_DTYPE)


def kernel(x_prompt, x_sample, cache_k, cache_v, cache_ik, state_hgrn, page_table, rel_bias,
           ln_g, w_in, q_norm_g, k_norm_g, hgrn_lb, hgrn_norm_g, w_out):
    b, s, d = x_prompt.shape
    db, dt, _ = x_sample.shape
    depth, n_pool = cache_k.shape[:2]
    n_pages = page_table.shape[1]
    past = n_pages * PAGE
    assert depth == 1 and dt == 1 and hgrn_lb.shape[0] == 2
    assert s % HG_BLK == 0 and s % FAR_CHUNK == 0 and s >= 4 * TOPK and past >= 4 * TOPK
    pps = min(MAX_PAGES_PER_STEP, n_pages)
    assert n_pages % pps == 0 and past % K_CHUNK == 0 and db % 8 == 0
    assert N_BUCKETS // 2 + int(math.log((PAGE + 1) / (N_BUCKETS // 2))
                                / math.log(MAX_DISTANCE / (N_BUCKETS // 2))
                                * (N_BUCKETS - N_BUCKETS // 2)) >= N_BUCKETS - 1

    w_pack = _pack_w_in(w_in[0])
    w_out_pack = jnp.concatenate(
        [_permute_heads(w_out[0][:D_ATTN], 0), w_out[0][D_ATTN:]], axis=0).astype(MXU_DTYPE)
    qg = jnp.tile(q_norm_g[0], A_HEADS)[None]
    kg = jnp.tile(k_norm_g[0], A_KV_HEADS)[None]
    hng = jnp.tile(hgrn_norm_g[0], H_HEADS)[None]
    bd = _block_diag(D_ATTN, HEAD_DIM, 1.0 / HEAD_DIM, MXU_DTYPE)
    lng = ln_g[0][None]

    pp = _project(x_prompt.reshape(b * s, d), lng, w_pack, qg, kg, hgrn_lb, bd, 256)
    r3 = lambda a: a.reshape(b, s, a.shape[-1])
    lp = -(-(s + Q_BLK) // K_CHUNK) * K_CHUNK
    padk = lambda a: jnp.pad(r3(a), ((0, 0), (Q_BLK, lp - s - Q_BLK), (0, 0)))
    vt = padk(pp["vb"]).transpose(0, 2, 1)
    ones = jnp.ones((b, V_ROWS - LANES, lp), vt.dtype)
    vtp = jnp.concatenate([vt[:, :LANES], ones, vt[:, LANES:], ones], axis=1)
    a_out = _attn_prompt(rel_bias, _bias_tiles(rel_bias), pp["qt"], pp["iqt"], pp["iwt"],
                         padk(pp["kb"]), vtp, padk(pp["ik2"]), b, s)
    h_out, st = _hgrn_prompt(pp["hq"], pp["hg"], pp["hk"], pp["hv"], b, s)
    y_prompt = _merge(x_prompt.reshape(b * s, d), a_out.reshape(b * s, D_ATTN), pp["sga"],
                      h_out.reshape(b * s, D_H), pp["sgh"], hng, bd, w_out_pack, 256)
    st = st.reshape(b, D_H // LANES, 2, H_KEY, 2, H_KEY)
    s_prompt = jnp.stack([st[:, :, e, :, e, :] for e in range(2)], axis=2)
    s_prompt = s_prompt.reshape(b, H_HEADS, H_KEY, H_KEY).transpose(0, 1, 3, 2)

    sp = _project(x_sample.reshape(db, d), lng, w_pack, qg, kg, hgrn_lb, bd, db)
    ik_s = sp["ikw"][:, :IDX_DIM]
    iw_s = sp["ikw"][:, IDX_DIM:IDX_DIM + IDX_HEADS]
    scores, s_new = _sample_scores(
        page_table, sp["iqt"].T.reshape(db, IDX_HEADS, IDX_DIM), iw_s.reshape(db, IDX_HEADS, 1),
        ik_s.reshape(db, 1, IDX_DIM), cache_ik[0].transpose(0, 2, 1),
        min(2 * MAX_PAGES_PER_STEP, n_pages))
    thr, cut = _sample_threshold(scores.reshape(db, past).T, s_new[:, :, 0].T)
    lane_b = lambda a: jnp.broadcast_to(a.reshape(db, 1, 1), (db, 1, LANES))
    q_s = sp["qt"].T
    qg4 = q_s.reshape(db, 2, 2, 2, HEAD_DIM)
    rows = []
    for p in range(2):
        g0lo, g0hi = qg4[:, p, 0, 0], qg4[:, p, 0, 1]
        g1lo, g1hi = qg4[:, p, 1, 0], qg4[:, p, 1, 1]
        z1 = jnp.zeros_like(g0lo)
        pr = jnp.stack([jnp.concatenate([g0lo, z1], -1), jnp.concatenate([g1lo, z1], -1),
                        jnp.concatenate([z1, g0hi], -1), jnp.concatenate([z1, g1hi], -1)]
                       + [jnp.zeros((db, LANES), q_s.dtype)] * 4, axis=1)
        rows.append(pr)
    zq = jnp.zeros_like(rows[0])
    qs = jnp.concatenate([jnp.concatenate([rows[0], zq], axis=-1),
                          jnp.concatenate([zq, rows[1]], axis=-1)], axis=1)
    tab_b = jnp.broadcast_to(rel_bias[:, :, None] * LOG2E, (N_BUCKETS, A_HEADS, LANES))
    page_t = lambda c: c[0].transpose(0, 2, 3, 1).reshape(n_pool, D_KV, PAGE)
    a_out_s = _sample_attention(
        page_table, tab_b, qs, scores, lane_b(thr), lane_b(cut), s_new,
        sp["kf"].reshape(db, 1, D_KV), sp["vf"].reshape(db, 1, D_KV),
        page_t(cache_k), page_t(cache_v), pps)
    unpair = lambda a: a.transpose(1, 0, 2).reshape(db, D_H)
    h_out_s, s_sample = _hgrn_sample(unpair(sp["hq"]), unpair(sp["hg"]), unpair(sp["hk"]),
                                     unpair(sp["hv"]),
                                     state_hgrn[0].transpose(1, 2, 3, 0))
    s_sample = s_sample.transpose(3, 0, 1, 2)
    y_sample = _merge(x_sample.reshape(db, d), a_out_s.reshape(db, D_ATTN), sp["sga"],
                      h_out_s.reshape(db, D_H), sp["sgh"], hng, bd, w_out_pack, db)

    kv5 = lambda a, n: a.reshape(1, n, -1, A_KV_HEADS, HEAD_DIM)
    return (y_prompt.reshape(b, s, d), y_sample.reshape(db, 1, d),
            kv5(pp["kf"], b), kv5(pp["vf"], b),
            pp["ikw"][:, :IDX_DIM].reshape(1, b, s, IDX_DIM), s_prompt[None],
            kv5(sp["kf"], db), kv5(sp["vf"], db),
            ik_s.reshape(1, db, 1, IDX_DIM), s_sample[None])
```

```python
import functools
import math

import jax
import jax.numpy as jnp
from jax import lax
from jax.experimental import pallas as pl
from jax.experimental.pallas import tpu as pltpu

F32 = jnp.float32
BF16 = jnp.bfloat16
MXU_DTYPE = BF16

HEAD_DIM = 64
A_HEADS = 8
A_KV_HEADS = 4
D_ATTN = A_HEADS * HEAD_DIM
D_KV = A_KV_HEADS * HEAD_DIM
IDX_HEADS = 8
IDX_DIM = 64
H_HEADS = 8
H_KEY = 64
D_H = H_HEADS * H_KEY
TOPK = 256
PAGE = 128
N_BUCKETS = 32
MAX_DISTANCE = 128
EPS = 1e-6
LANES = 128
Q_BLK = 256
K_CHUNK = 512
FAR_CHUNK = 1024
HG_BLK = 256
HG_CHUNK = 16
V_ROWS = LANES + 16
MAX_PAGES_PER_STEP = 32
VMEM_LIMIT = 56 * 1024 * 1024
NEG_INF = float("-inf")
LOG2E = math.log2(math.e)
KEY_NEG_INF = -2139095041

C_AQ, C_AK, C_AV, C_AG, C_IQ, C_IK, C_IW, C_HQ, C_HF, C_HI, C_HG, C_END = (
    0, 512, 768, 1024, 1536, 2048, 2176, 2304, 2816, 3328, 3840, 4352)
HEAD_PERM = (0, 2, 1, 3, 4, 6, 5, 7)


def _mxu(x):
    return x.astype(MXU_DTYPE)


def _dot(a, b):
    return jnp.dot(a, b, preferred_element_type=F32)


def _dot_nt(a, b):
    return lax.dot_general(a, b, (((1,), (1,)), ((), ())), preferred_element_type=F32)


def _dot_tn(a, b):
    return lax.dot_general(a, b, (((0,), (0,)), ((), ())), preferred_element_type=F32)


def _split_dot(x, w, parts):
    acc = None
    r = x
    for _ in range(parts):
        p = _mxu(r)
        t = _dot(p, w)
        acc = t if acc is None else acc + t
        r = r - p.astype(F32)
    return acc


def _sigmoid(x):
    return 1.0 / (1.0 + jnp.exp(-x))


def _silu(x):
    return x * _sigmoid(x)


def _t5_bucket(n):
    n = jnp.maximum(n, 0)
    max_exact = N_BUCKETS // 2
    nf = jnp.maximum(n, 1).astype(F32)
    large = max_exact + jnp.floor(jnp.log(nf / max_exact) / math.log(MAX_DISTANCE / max_exact)
                                  * (N_BUCKETS - max_exact)).astype(jnp.int32)
    large = jnp.minimum(large, N_BUCKETS - 1)
    return jnp.where(n < max_exact, n, large)


def _block_diag(n, blk, val, dtype):
    r = lax.broadcasted_iota(jnp.int32, (n, n), 0) // blk
    c = lax.broadcasted_iota(jnp.int32, (n, n), 1) // blk
    return jnp.where(r == c, val, 0.0).astype(dtype)


def _proj_kernel(x_ref, lng_ref, w_ref, qg_ref, kg_ref, lb_ref, bd_ref,
                 qt_ref, iqt_ref, iwt_ref, kf_ref, kb_ref, vf_ref, vb_ref, ikw_ref, ik2_ref,
                 sga_ref, hq_ref, hg_ref, hk_ref, hv_ref, sgh_ref):
    x = x_ref[...]
    ms = jnp.mean(x * x, axis=-1, keepdims=True)
    xb = _mxu(x * lax.rsqrt(ms + EPS) * lng_ref[...])

    def seg(a, b):
        return _dot(xb, w_ref[:, a:b])

    bd = bd_ref[...]

    aq = seg(C_AQ, C_AK)
    msq = _split_dot(aq * aq, bd, 2)
    qt_ref[...] = (aq * lax.rsqrt(msq + EPS) * qg_ref[...]
                   * (HEAD_DIM ** -0.5 * LOG2E)).T.astype(qt_ref.dtype)

    ak = seg(C_AK, C_AV)
    msk = _split_dot(ak * ak, bd[:D_KV, :D_KV], 2)
    k = ak * lax.rsqrt(msk + EPS) * kg_ref[...]
    kf_ref[...] = k
    kb_ref[...] = k.astype(kb_ref.dtype)

    v = seg(C_AV, C_AG)
    vf_ref[...] = v
    vb_ref[...] = v.astype(vb_ref.dtype)

    sga_ref[...] = _silu(seg(C_AG, C_IQ))
    iqt_ref[...] = seg(C_IQ, C_IK).T.astype(iqt_ref.dtype)

    ikk = seg(C_IK, C_IW)
    iww = seg(C_IW, C_HQ) * (IDX_HEADS ** -0.5 * IDX_DIM ** -0.5)
    ik2_ref[...] = ikk.astype(ik2_ref.dtype)
    iwt_ref[...] = iww.T[IDX_DIM:IDX_DIM + IDX_HEADS]
    lane = lax.broadcasted_iota(jnp.int32, ikk.shape, 1)
    ikw_ref[...] = jnp.where(lane < IDX_DIM, ikk, iww)

    def put_pairs(ref, val):
        for p in range(D_H // LANES):
            ref[p] = val[:, p * LANES:(p + 1) * LANES]

    put_pairs(hq_ref, _silu(seg(C_HQ, C_HF)) * H_KEY ** -0.5)
    lbp = lb_ref[...]
    mx = jnp.max(lbp, axis=0, keepdims=True)
    e = jnp.exp(lbp - mx)
    lb = e[0:1] / jnp.sum(e, axis=0, keepdims=True)
    f = lb + (1.0 - lb) * _sigmoid(seg(C_HF, C_HI))
    put_pairs(hg_ref, jnp.log(f))
    put_pairs(hk_ref, 1.0 - f)
    put_pairs(hv_ref, seg(C_HI, C_HG))
    sgh_ref[...] = _silu(seg(C_HG, C_END))


def _project(x2d, ln_g, w_pack, qg, kg, lbp, bd, tm):
    n, d = x2d.shape
    row = lambda w: pl.BlockSpec((tm, w), lambda i: (i, 0))
    full = lambda a: pl.BlockSpec(a.shape, lambda i: (0,) * a.ndim)
    outs = [("qt", -D_ATTN, MXU_DTYPE), ("iqt", -D_ATTN, MXU_DTYPE), ("iwt", -IDX_HEADS, F32),
            ("kf", D_KV, F32),
            ("kb", D_KV, MXU_DTYPE), ("vf", D_KV, F32), ("vb", D_KV, MXU_DTYPE),
            ("ikw", LANES, F32), ("ik2", LANES, MXU_DTYPE), ("sga", D_ATTN, F32),
            ("hq", 0, F32), ("hg", 0, F32), ("hk", 0, F32), ("hv", 0, F32),
            ("sgh", D_H, F32)]
    n_pairs = D_H // LANES

    def spec(w):
        if w > 0:
            return row(w), (n, w)
        if w < 0:
            return pl.BlockSpec((-w, tm), lambda i: (0, i)), (-w, n)
        return pl.BlockSpec((n_pairs, tm, LANES), lambda i: (0, i, 0)), (n_pairs, n, LANES)

    res = pl.pallas_call(
        _proj_kernel,
        grid=(n // tm,),
        in_specs=[row(d), full(ln_g), full(w_pack), full(qg), full(kg), full(lbp), full(bd)],
        out_specs=[spec(w)[0] for _, w, _ in outs],
        out_shape=[jax.ShapeDtypeStruct(spec(w)[1], dt) for _, w, dt in outs],
        compiler_params=pltpu.CompilerParams(
            dimension_semantics=("arbitrary",), vmem_limit_bytes=VMEM_LIMIT),
        name="proj",
    )(x2d, ln_g, w_pack, qg, kg, lbp, bd)
    return dict(zip([o[0] for o in outs], res))


def _bias_tile_kernel(tab_ref, o_ref):
    j = lax.broadcasted_iota(jnp.int32, (2 * Q_BLK, Q_BLK), 0)
    r = lax.broadcasted_iota(jnp.int32, (2 * Q_BLK, Q_BLK), 1)
    bucket = _t5_bucket(Q_BLK + r - j)
    for h in range(A_HEADS):
        acc = jnp.zeros((2 * Q_BLK, Q_BLK), F32)
        for b in range(N_BUCKETS):
            acc = jnp.where(bucket == b, tab_ref[b, h] * LOG2E, acc)
        o_ref[h] = acc


def _bias_tiles(rel_bias):
    return pl.pallas_call(
        _bias_tile_kernel,
        in_specs=[pl.BlockSpec(memory_space=pltpu.SMEM)],
        out_shape=jax.ShapeDtypeStruct((A_HEADS, 2 * Q_BLK, Q_BLK), F32),
        name="bias_tiles",
    )(rel_bias)


def _key_to_f32(key):
    bits = key ^ ((key >> 31) & jnp.int32(0x7FFFFFFF))
    return lax.bitcast_convert_type(bits, F32)


def _topk_cut(sc_ref, nch, cut_ref, tie_ref, extra=None):
    ncol = sc_ref.shape[1]
    nrows = nch * K_CHUNK
    nbits = int(sc_ref.shape[0]).bit_length()
    groups = K_CHUNK // 8

    def count(pred, ref, extra_val, side=None):
        def body(c, acc):
            r0 = pl.multiple_of(c * K_CHUNK, K_CHUNK)
            v = ref[pl.ds(r0, K_CHUNK), :]
            if side is not None:
                side(v, r0)
            hit = jnp.where(pred(v), 1, 0).astype(jnp.int32).reshape(8, groups // 8, 8, ncol)
            parts = [jnp.sum(hit[g], axis=0, dtype=jnp.int32) for g in range(8)]
            return acc + (((parts[0] + parts[1]) + (parts[2] + parts[3]))
                          + ((parts[4] + parts[5]) + (parts[6] + parts[7])))

        acc = lax.fori_loop(0, nch, body, jnp.zeros((8, ncol), jnp.int32))
        cnt = jnp.sum(acc, axis=0, keepdims=True, dtype=jnp.int32)
        if extra_val is not None:
            cnt = cnt + jnp.where(pred(extra_val), 1, 0).astype(jnp.int32)
        return cnt

    def bit_body(it, carry):
        key, n_ge = carry
        cand = key + jnp.left_shift(jnp.int32(1), 31 - it)
        cf = _key_to_f32(cand)
        cnt = count(lambda v: v >= cf, sc_ref, extra)
        ok = cnt >= TOPK
        return jnp.where(ok, cand, key), jnp.where(ok, cnt, n_ge)

    key0 = jnp.full((1, ncol), jnp.iinfo(jnp.int32).min, jnp.int32)
    key, n_ge = lax.fori_loop(0, 32, bit_body, (key0, jnp.full((1, ncol), TOPK, jnp.int32)))
    below = key < KEY_NEG_INF
    thr = _key_to_f32(jnp.maximum(key, KEY_NEG_INF))
    cut_ref[...] = jnp.full((1, ncol), jnp.iinfo(jnp.int32).max, jnp.int32)

    @pl.when(jnp.max(jnp.where(below, TOPK + 1, n_ge)) > TOPK)
    def _():
        big = jnp.int32(1 << 30)

        def mark(v, r0):
            row = r0 + lax.broadcasted_iota(jnp.int32, v.shape, 0)
            tie_ref[pl.ds(r0, K_CHUNK), :] = jnp.where(v == thr, row, big)

        need = TOPK - count(lambda v: v > thr, sc_ref, extra, side=mark)
        extra_tie = None if extra is None else jnp.where(extra == thr, nrows, big)

        def idx_body(it, x):
            cand = x + jnp.left_shift(jnp.int32(1), nbits - 1 - it)
            g = count(lambda t: t < cand, tie_ref, extra_tie)
            return jnp.where(g < need, cand, x)

        cut_ref[...] = lax.fori_loop(0, nbits, idx_body, jnp.zeros((1, ncol), jnp.int32))

    return thr


def _in_topk(sc, col, thr, cut):
    return (sc > thr) | ((sc == thr) & (col <= cut))


def _row_halves(x):
    half = x.shape[0] // 2
    zero = jnp.zeros((half, x.shape[1]), x.dtype)
    return (jnp.concatenate([x[:half], zero], axis=0), jnp.concatenate([zero, x[half:]], axis=0))


def _attn_prompt_kernel(tab_ref, qt_ref, iqt_ref, iwt_ref, kb_ref, vt_ref, ik2_ref, bt_ref,
                        o_ref, sc_ref, tie_ref, cut_ref, m_ref, acc_ref):
    i = pl.program_id(1)
    t_pos = i * Q_BLK + lax.broadcasted_iota(jnp.int32, (1, Q_BLK), 1)

    @pl.when((pl.program_id(0) == 0) & (i == 0))
    def _():
        sc_ref[...] = jnp.full(sc_ref.shape, NEG_INF, F32)

    iqt = iqt_ref[...]
    cols = []
    for g in range(D_ATTN // LANES):
        cols.extend(_row_halves(iqt[g * LANES:(g + 1) * LANES]))
    iq_stack = jnp.concatenate(cols, axis=1)
    iw = iwt_ref[...]
    nch = ((i + 2) * Q_BLK + K_CHUNK - 1) // K_CHUNK

    def score_body(c, carry):
        r0 = pl.multiple_of(c * K_CHUNK, K_CHUNK)
        s_all = _dot(ik2_ref[0, pl.ds(r0, K_CHUNK), :], iq_stack)
        sc = jnp.zeros((K_CHUNK, Q_BLK), F32)
        for h in range(IDX_HEADS):
            sc = sc + jnp.maximum(s_all[:, h * Q_BLK:(h + 1) * Q_BLK], 0.0) * iw[h:h + 1]
        s_glob = r0 - Q_BLK + lax.broadcasted_iota(jnp.int32, (K_CHUNK, Q_BLK), 0)
        valid = (s_glob >= 0) & (s_glob <= t_pos)
        sc_ref[pl.ds(r0, K_CHUNK), :] = jnp.where(valid, sc, NEG_INF)
        return carry

    lax.fori_loop(0, nch, score_body, 0)
    thr = _topk_cut(sc_ref, nch, cut_ref, tie_ref)
    cut = cut_ref[...]

    m_ref[...] = jnp.full(m_ref.shape, NEG_INF, F32)
    acc_ref[...] = jnp.zeros(acc_ref.shape, F32)

    qt = qt_ref[...]
    q_pairs = []
    for p in range(2):
        g0 = _row_halves(qt[(2 * p) * LANES:(2 * p + 1) * LANES])
        g1 = _row_halves(qt[(2 * p + 1) * LANES:(2 * p + 2) * LANES])
        q_pairs.append(jnp.concatenate([g0[0], g1[0], g0[1], g1[1]], axis=1))

    def attend(r0, width, valid, bias_fn, const_fn):
        madd = jnp.where(valid, 0.0, NEG_INF)
        for p in range(2):
            kc = kb_ref[0, pl.ds(r0, width), p * LANES:(p + 1) * LANES]
            vc = vt_ref[0, p * V_ROWS:(p + 1) * V_ROWS, pl.ds(r0, width)]
            st = _dot(kc, q_pairs[p])
            ps, alphas = [], []
            for a in range(4):
                h = 4 * p + a
                x = st[:, a * Q_BLK:(a + 1) * Q_BLK] + madd
                bias = bias_fn(h)
                if bias is not None:
                    x = x + bias
                c = const_fn(h)
                m_old = m_ref[h:h + 1]
                m_new = jnp.maximum(m_old, jnp.max(x, axis=0, keepdims=True) + c)
                m_safe = jnp.where(m_new == NEG_INF, 0.0, m_new)
                m_ref[h:h + 1] = m_new
                ps.append(_mxu(jnp.exp2(x - (m_safe - c))))
                alphas.append(jnp.exp2(m_old - m_safe))
            pv = _dot(vc, jnp.concatenate(ps, axis=1))
            acc_ref[p] = acc_ref[p] * jnp.concatenate(alphas, axis=1) + pv

    def far_body(c, carry):
        r0 = pl.multiple_of(c * FAR_CHUNK, FAR_CHUNK)
        row = r0 + lax.broadcasted_iota(jnp.int32, (FAR_CHUNK, Q_BLK), 0)
        valid = ((row >= Q_BLK) & (row < i * Q_BLK)
                 & _in_topk(sc_ref[pl.ds(r0, FAR_CHUNK), :], row, thr, cut))
        attend(r0, FAR_CHUNK, valid, lambda h: None, lambda h: tab_ref[N_BUCKETS - 1, h] * LOG2E)
        return carry

    lax.fori_loop(0, (i * Q_BLK + FAR_CHUNK - 1) // FAR_CHUNK, far_body, 0)

    r0 = pl.multiple_of(i * Q_BLK, Q_BLK)
    row = r0 + lax.broadcasted_iota(jnp.int32, (2 * Q_BLK, Q_BLK), 0)
    valid = ((row >= Q_BLK) & (row - Q_BLK <= t_pos)
             & _in_topk(sc_ref[pl.ds(r0, 2 * Q_BLK), :], row, thr, cut))
    attend(r0, 2 * Q_BLK, valid, lambda h: bt_ref[h], lambda h: 0.0)

    top = lax.broadcasted_iota(jnp.int32, (LANES, Q_BLK), 0) < HEAD_DIM
    for p in range(2):
        acc = acc_ref[p]
        outs = [acc[:LANES, a * Q_BLK:(a + 1) * Q_BLK] / acc[LANES:LANES + 1, a * Q_BLK:(a + 1) * Q_BLK]
                for a in range(4)]
        o_ref[0, :, (2 * p) * LANES:(2 * p + 1) * LANES] = jnp.where(top, outs[0], outs[2]).T
        o_ref[0, :, (2 * p + 1) * LANES:(2 * p + 2) * LANES] = jnp.where(top, outs[1], outs[3]).T


def _attn_prompt(rel_bias, bias_tiles, qt, iqt, iwt, kbp, vtp, ik2p, b, s):
    nq = s // Q_BLK
    lp = kbp.shape[1]
    blk_t = lambda r: pl.BlockSpec((r, Q_BLK), lambda bi, i: (0, bi * nq + i))
    return pl.pallas_call(
        _attn_prompt_kernel,
        grid=(b, s // Q_BLK),
        in_specs=[pl.BlockSpec(memory_space=pltpu.SMEM),
                  blk_t(D_ATTN), blk_t(D_ATTN), blk_t(IDX_HEADS),
                  pl.BlockSpec((1, lp, D_KV), lambda bi, i: (bi, 0, 0)),
                  pl.BlockSpec((1, 2 * V_ROWS, lp), lambda bi, i: (bi, 0, 0)),
                  pl.BlockSpec((1, lp, LANES), lambda bi, i: (bi, 0, 0)),
                  pl.BlockSpec(bias_tiles.shape, lambda bi, i: (0, 0, 0))],
        out_specs=pl.BlockSpec((1, Q_BLK, D_ATTN), lambda bi, i: (bi, i, 0)),
        out_shape=jax.ShapeDtypeStruct((b, s, D_ATTN), F32),
        scratch_shapes=[pltpu.VMEM((lp, Q_BLK), F32),
                        pltpu.VMEM((lp, Q_BLK), jnp.int32),
                        pltpu.VMEM((1, Q_BLK), jnp.int32),
                        pltpu.VMEM((A_HEADS, Q_BLK), F32),
                        pltpu.VMEM((2, V_ROWS, 4 * Q_BLK), F32)],
        compiler_params=pltpu.CompilerParams(
            dimension_semantics=("arbitrary", "arbitrary"), vmem_limit_bytes=VMEM_LIMIT),
        name="attn_prompt",
    )(rel_bias, qt, iqt, iwt, kbp, vtp, ik2p, bias_tiles)


def _hgrn_prompt_kernel(q_ref, g_ref, k_ref, v_ref, o_ref, st_ref, b_ref, oi_ref,
                        tq_ref, tk_ref, tv_ref, tb_ref):
    j = pl.program_id(1)
    n_pairs = D_H // LANES
    n_chunks = HG_BLK // HG_CHUNK

    @pl.when(j == 0)
    def _():
        st_ref[...] = jnp.zeros(st_ref.shape, F32)

    def lanes(ref, rows):
        return jnp.concatenate([ref[p, rows, :] for p in range(n_pairs)], axis=1)

    r = lax.broadcasted_iota(jnp.int32, (HG_BLK, HG_BLK), 0)
    c = lax.broadcasted_iota(jnp.int32, (HG_BLK, HG_BLK), 1)
    tri = jnp.where(c <= r, 1.0, 0.0).astype(MXU_DTYPE)
    bcum = _split_dot_left(tri, lanes(g_ref, slice(None)), 3)
    for p in range(n_pairs):
        b_ref[p] = bcum[:, p * LANES:(p + 1) * LANES]

    for t in range(HG_CHUNK):
        rows = pl.ds(t, n_chunks, stride=HG_CHUNK)
        tq_ref[t] = lanes(q_ref, rows)
        tk_ref[t] = lanes(k_ref, rows)
        tv_ref[t] = lanes(v_ref, rows)
        tb_ref[t] = lanes(b_ref, rows)

    half = D_H // 2
    bd_half = _block_diag(half, H_KEY, 1.0, MXU_DTYPE)
    for t in range(HG_CHUNK):
        qt = tq_ref[t]
        bt = tb_ref[t]
        parts = [qt * tk_ref[s] * jnp.exp(bt - tb_ref[s]) for s in range(t)]
        parts.append(qt * tk_ref[t])
        pr = _mxu(jnp.concatenate(parts, axis=0))
        rs = jnp.concatenate([_dot(pr[:, :half], bd_half), _dot(pr[:, half:], bd_half)], axis=1)
        acc = rs[0:n_chunks] * tv_ref[0]
        for s in range(1, t + 1):
            acc = acc + rs[s * n_chunks:(s + 1) * n_chunks] * tv_ref[s]
        for p in range(n_pairs):
            oi_ref[p, pl.ds(t, n_chunks, stride=HG_CHUNK), :] = acc[:, p * LANES:(p + 1) * LANES]

    lane_r = lax.broadcasted_iota(jnp.int32, (LANES, LANES), 0) // H_KEY
    lane_c = lax.broadcasted_iota(jnp.int32, (LANES, LANES), 1) // H_KEY
    same_head = lane_r == lane_c
    for cch in range(n_chunks):
        rows = slice(cch * HG_CHUNK, (cch + 1) * HG_CHUNK)
        last = slice((cch + 1) * HG_CHUNK - 1, (cch + 1) * HG_CHUNK)
        prev = slice(cch * HG_CHUNK - 1, cch * HG_CHUNK)
        outs = []
        for p in range(n_pairs):
            bch = b_ref[p, rows, :]
            b0 = b_ref[p, prev, :] if cch > 0 else jnp.zeros((1, LANES), F32)
            bl = b_ref[p, last, :]
            qi = _mxu(q_ref[p, rows, :] * jnp.exp(bch - b0))
            kl = _mxu(k_ref[p, rows, :] * jnp.exp(bl - bch))
            st = st_ref[0, p]
            outs.append(oi_ref[p, rows, :] + _dot_nt(qi, _mxu(st)))
            ds = _dot_tn(_mxu(v_ref[p, rows, :]), kl)
            st_ref[0, p] = st * jnp.exp(bl - b0) + jnp.where(same_head, ds, 0.0)
        o_ref[0, rows, :] = jnp.concatenate(outs, axis=1)


def _split_dot_left(w, x, parts):
    acc = None
    r = x
    for _ in range(parts):
        p = _mxu(r)
        t = _dot(w, p)
        acc = t if acc is None else acc + t
        r = r - p.astype(F32)
    return acc


def _hgrn_prompt(hq, hg, hk, hv, b, s):
    n_pairs = D_H // LANES
    nblk = s // HG_BLK
    blk = pl.BlockSpec((n_pairs, HG_BLK, LANES), lambda bi, j: (0, bi * nblk + j, 0))
    return pl.pallas_call(
        _hgrn_prompt_kernel,
        grid=(b, nblk),
        in_specs=[blk, blk, blk, blk],
        out_specs=[pl.BlockSpec((1, HG_BLK, D_H), lambda bi, j: (bi, j, 0)),
                   pl.BlockSpec((1, n_pairs, LANES, LANES), lambda bi, j: (bi, 0, 0, 0))],
        out_shape=[jax.ShapeDtypeStruct((b, s, D_H), F32),
                   jax.ShapeDtypeStruct((b, n_pairs, LANES, LANES), F32)],
        scratch_shapes=[pltpu.VMEM((n_pairs, HG_BLK, LANES), F32)] * 2
        + [pltpu.VMEM((HG_CHUNK, HG_BLK // HG_CHUNK, D_H), F32)] * 4,
        compiler_params=pltpu.CompilerParams(
            dimension_semantics=("arbitrary", "arbitrary"), vmem_limit_bytes=VMEM_LIMIT),
        name="hgrn_prompt",
    )(hq, hg, hk, hv)


def _hgrn_sample_kernel(q_ref, g_ref, k_ref, v_ref, s_ref, o_ref, sn_ref):
    q = q_ref[...]
    e = jnp.exp(g_ref[...])
    k = k_ref[...]
    v = v_ref[...]
    o = jnp.zeros(v.shape, F32)
    for kk in range(H_KEY):
        sn = e[kk:kk + 1] * s_ref[0, kk] + k[kk:kk + 1] * v
        sn_ref[0, kk] = sn
        o = o + q[kk:kk + 1] * sn
    o_ref[...] = o


def _hgrn_sample(hq, hg, hk, hv, state_t):
    db = hq.shape[0]
    hspec = pl.BlockSpec((H_KEY, db), lambda h: (h, 0))
    sspec = pl.BlockSpec((1, H_KEY, H_KEY, db), lambda h: (h, 0, 0, 0))
    o_t, s_new = pl.pallas_call(
        _hgrn_sample_kernel,
        grid=(H_HEADS,),
        in_specs=[hspec, hspec, hspec, hspec, sspec],
        out_specs=[hspec, sspec],
        out_shape=[jax.ShapeDtypeStruct((D_H, db), F32),
                   jax.ShapeDtypeStruct(state_t.shape, F32)],
        compiler_params=pltpu.CompilerParams(
            dimension_semantics=("arbitrary",), vmem_limit_bytes=VMEM_LIMIT),
        name="hgrn_sample",
    )(hq.T, hg.T, hk.T, hv.T, state_t)
    return o_t.T, s_new


def _merge_kernel(x_ref, a_ref, sga_ref, h_ref, sgh_ref, hng_ref, bd_ref, w_ref, y_ref):
    h = h_ref[...]
    msq = _split_dot(h * h, bd_ref[...], 2)
    hn = h * lax.rsqrt(msq + EPS) * hng_ref[...]
    ma = _mxu(a_ref[...] * sga_ref[...])
    mh = _mxu(hn * sgh_ref[...])
    y_ref[...] = x_ref[...] + _dot(ma, w_ref[0:D_ATTN, :]) + _dot(mh, w_ref[D_ATTN:, :])


def _merge(x2d, a_out, sga, h_out, sgh, hng, bd, w_pack, tm):
    n, d = x2d.shape
    row = lambda w: pl.BlockSpec((tm, w), lambda i: (i, 0))
    full = lambda a: pl.BlockSpec(a.shape, lambda i: (0,) * a.ndim)
    return pl.pallas_call(
        _merge_kernel,
        grid=(n // tm,),
        in_specs=[row(d), row(D_ATTN), row(D_ATTN), row(D_H), row(D_H),
                  full(hng), full(bd), full(w_pack)],
        out_specs=row(d),
        out_shape=jax.ShapeDtypeStruct((n, d), F32),
        compiler_params=pltpu.CompilerParams(
            dimension_semantics=("arbitrary",), vmem_limit_bytes=VMEM_LIMIT),
        name="merge",
    )(x2d, a_out, sga, h_out, sgh, hng, bd, w_pack)


def _sample_score_kernel(pps, pt_ref, iqh_ref, iwc_ref, ikn_ref, cik_ref, sc_ref, sn_ref,
                         ikbuf, sem):
    b = pl.program_id(0)
    j = pl.program_id(1)
    nj = pl.num_programs(1)
    step = b * nj + j
    slot = step % 2

    def page_copies(bb, jj, sl):
        return [pltpu.make_async_copy(cik_ref.at[pt_ref[bb, jj * pps + r]], ikbuf.at[sl, r],
                                      sem.at[sl]) for r in range(pps)]

    @pl.when(step == 0)
    def _():
        for cp in page_copies(0, 0, 0):
            cp.start()

    @pl.when(step + 1 < pl.num_programs(0) * nj)
    def _():
        nxt = step + 1
        for cp in page_copies(nxt // nj, nxt % nj, 1 - slot):
            cp.start()

    for cp in page_copies(b, j, slot):
        cp.wait()
    iqh = iqh_ref[0]
    iwc = iwc_ref[0]
    ikt = jnp.concatenate([_mxu(ikbuf[slot, r]) for r in range(pps)], axis=1)
    s = _dot(iqh, ikt)
    sc_ref[0] = jnp.sum(jnp.maximum(s, 0.0) * iwc, axis=0, keepdims=True)
    prod = iqh.astype(F32) * _mxu(ikn_ref[0]).astype(F32)
    s_new = jnp.sum(prod, axis=1, keepdims=True)
    s_new = jnp.sum(jnp.maximum(s_new, 0.0) * iwc, axis=0, keepdims=True)
    sn_ref[0] = jnp.broadcast_to(s_new, (1, LANES))


def _sample_scores(page_table, iqh, iwc, ikn, cache_ikt, pps):
    db, n_pages = page_table.shape
    gs = pltpu.PrefetchScalarGridSpec(
        num_scalar_prefetch=1,
        grid=(db, n_pages // pps),
        in_specs=[pl.BlockSpec((1, IDX_HEADS, IDX_DIM), lambda b, j, pt: (b, 0, 0)),
                  pl.BlockSpec((1, IDX_HEADS, 1), lambda b, j, pt: (b, 0, 0)),
                  pl.BlockSpec((1, 1, IDX_DIM), lambda b, j, pt: (b, 0, 0)),
                  pl.BlockSpec(memory_space=pl.ANY)],
        out_specs=[pl.BlockSpec((1, 1, pps * PAGE), lambda b, j, pt: (b, 0, j)),
                   pl.BlockSpec((1, 1, LANES), lambda b, j, pt: (b, 0, 0))],
        scratch_shapes=[pltpu.VMEM((2, pps, IDX_DIM, PAGE), F32),
                        pltpu.SemaphoreType.DMA((2,))],
    )
    return pl.pallas_call(
        functools.partial(_sample_score_kernel, pps),
        grid_spec=gs,
        out_shape=[jax.ShapeDtypeStruct((db, 1, n_pages * PAGE), F32),
                   jax.ShapeDtypeStruct((db, 1, LANES), F32)],
        compiler_params=pltpu.CompilerParams(
            dimension_semantics=("arbitrary", "arbitrary"), vmem_limit_bytes=VMEM_LIMIT),
        name="sample_scores",
    )(page_table, iqh, iwc, ikn, cache_ikt)


def _sample_thr_kernel(sc_ref, sn_ref, thr_ref, cut_ref, tie_ref):
    past = sc_ref.shape[0]
    thr_ref[...] = _topk_cut(sc_ref, past // K_CHUNK, cut_ref, tie_ref, extra=sn_ref[...])


def _sample_threshold(scores_t, s_new_t):
    db = scores_t.shape[1]
    return pl.pallas_call(
        _sample_thr_kernel,
        out_shape=[jax.ShapeDtypeStruct((1, db), F32),
                   jax.ShapeDtypeStruct((1, db), jnp.int32)],
        scratch_shapes=[pltpu.VMEM(scores_t.shape, jnp.int32)],
        compiler_params=pltpu.CompilerParams(vmem_limit_bytes=VMEM_LIMIT),
        name="sample_threshold",
    )(scores_t, s_new_t)


def _sample_attn_kernel(pps, pt_ref, tb_ref, qs_ref, sc_ref, thr_ref, cut_ref, sn_ref,
                        kn_ref, vn_ref, ck_ref, cv_ref, o_ref,
                        kbuf, vbuf, sem, m_ref, l_ref, acc_ref):
    b = pl.program_id(0)
    j = pl.program_id(1)
    nb = pl.num_programs(0)
    nj = pl.num_programs(1)
    n = pps * PAGE
    past = nj * n
    step = b * nj + j
    slot = step % 2

    def page_copies(bb, jj, sl):
        copies = []
        for r in range(pps):
            page = pt_ref[bb, jj * pps + r]
            copies.append(pltpu.make_async_copy(ck_ref.at[page], kbuf.at[sl, r], sem.at[0, sl]))
            copies.append(pltpu.make_async_copy(cv_ref.at[page], vbuf.at[sl, r], sem.at[1, sl]))
        return copies

    @pl.when(step == 0)
    def _():
        for cp in page_copies(0, 0, 0):
            cp.start()

    @pl.when(step + 1 < nb * nj)
    def _():
        nxt = step + 1
        for cp in page_copies(nxt // nj, nxt % nj, 1 - slot):
            cp.start()

    for cp in page_copies(b, j, slot):
        cp.wait()
    k_refs = [kbuf.at[slot, r] for r in range(pps)]
    v_refs = [vbuf.at[slot, r] for r in range(pps)]

    @pl.when(j == 0)
    def _():
        m_ref[...] = jnp.full(m_ref.shape, NEG_INF, F32)
        l_ref[...] = jnp.zeros(l_ref.shape, F32)
        acc_ref[...] = jnp.zeros(acc_ref.shape, F32)

    thr = thr_ref[0, :, 0:1]
    cut = cut_ref[0, :, 0:1]
    qs = qs_ref[0]
    lo = lax.broadcasted_iota(jnp.int32, (1, LANES), 1) < HEAD_DIM

    def update(x, pv_fn):
        m_old = m_ref[...]
        m_new = jnp.maximum(m_old, jnp.max(x, axis=1, keepdims=True))
        m_safe = jnp.where(m_new == NEG_INF, 0.0, m_new)
        alpha = jnp.exp2(m_old - m_safe)
        pe = jnp.exp2(x - m_safe)
        l_ref[...] = alpha * l_ref[...] + jnp.sum(pe, axis=1, keepdims=True)
        m_ref[...] = m_new
        acc_ref[...] = alpha * acc_ref[...] + pv_fn(_mxu(pe))

    def head_rows(b8):
        z = jnp.zeros((4, b8.shape[1]), F32)
        return jnp.concatenate([b8[0:4], z, b8[4:8], z], axis=0)

    s_idx = j * n + lax.broadcasted_iota(jnp.int32, (1, n), 1)
    s_last = s_idx[:, n - PAGE:]
    bucket = _t5_bucket(past - s_last)
    near = jnp.zeros((A_HEADS, PAGE), F32)
    for b in range(N_BUCKETS):
        near = jnp.where(bucket == b, tb_ref[b], near)
    far = tb_ref[N_BUCKETS - 1]
    bias8 = jnp.concatenate([far] * (pps - 1) + [jnp.where(j == nj - 1, near, far)], axis=1)
    maskadd = jnp.where(_in_topk(sc_ref[0], s_idx, thr, cut), 0.0, NEG_INF)
    kt = jnp.concatenate([_mxu(r[...]) for r in k_refs], axis=1)
    vt = jnp.concatenate([_mxu(r[...]) for r in v_refs], axis=1)
    update(_dot(qs, kt) + head_rows(bias8) + maskadd, lambda pe: _dot_nt(pe, vt))

    @pl.when(j == nj - 1)
    def _():
        valid = (_in_topk(sn_ref[0], past, thr, cut)
                 & (lax.broadcasted_iota(jnp.int32, (1, LANES), 1) == 0))
        madd = jnp.where(valid, 0.0, NEG_INF)
        kn = _mxu(jnp.broadcast_to(kn_ref[0], (LANES, D_KV)))
        vn = _mxu(jnp.broadcast_to(vn_ref[0], (LANES, D_KV)))
        update(_dot_nt(qs, kn) + head_rows(tb_ref[0]) + madd, lambda pe: _dot(pe, vn))
        out = acc_ref[...] / l_ref[...]
        for p in range(2):
            o = out[8 * p:8 * p + 4, p * LANES:(p + 1) * LANES]
            o_ref[0, :, (2 * p) * LANES:(2 * p + 1) * LANES] = jnp.where(lo, o[0:1], o[2:3])
            o_ref[0, :, (2 * p + 1) * LANES:(2 * p + 2) * LANES] = jnp.where(lo, o[1:2], o[3:4])


def _sample_attention(page_table, tab_b, qs, scores, thr, cut, s_new, k_new, v_new,
                      cache_kt, cache_vt, pps):
    db, n_pages = page_table.shape
    per_seq = lambda shape: pl.BlockSpec((1,) + shape, lambda b, j, pt: (b,) + (0,) * len(shape))
    gs = pltpu.PrefetchScalarGridSpec(
        num_scalar_prefetch=1,
        grid=(db, n_pages // pps),
        in_specs=[pl.BlockSpec(tab_b.shape, lambda b, j, pt: (0, 0, 0)),
                  per_seq((2 * A_HEADS, D_KV)),
                  pl.BlockSpec((1, 1, pps * PAGE), lambda b, j, pt: (b, 0, j)),
                  per_seq((1, LANES)), per_seq((1, LANES)), per_seq((1, LANES)),
                  per_seq((1, D_KV)), per_seq((1, D_KV)),
                  pl.BlockSpec(memory_space=pl.ANY), pl.BlockSpec(memory_space=pl.ANY)],
        out_specs=per_seq((1, D_ATTN)),
        scratch_shapes=[pltpu.VMEM((2, pps, D_KV, PAGE), F32),
                        pltpu.VMEM((2, pps, D_KV, PAGE), F32),
                        pltpu.SemaphoreType.DMA((2, 2)),
                        pltpu.VMEM((2 * A_HEADS, 1), F32),
                        pltpu.VMEM((2 * A_HEADS, 1), F32),
                        pltpu.VMEM((2 * A_HEADS, D_KV), F32)],
    )
    return pl.pallas_call(
        functools.partial(_sample_attn_kernel, pps),
        grid_spec=gs,
        out_shape=jax.ShapeDtypeStruct((db, 1, D_ATTN), F32),
        compiler_params=pltpu.CompilerParams(
            dimension_semantics=("arbitrary", "arbitrary"), vmem_limit_bytes=VMEM_LIMIT),
        name="sample_attn",
    )(page_table, tab_b, qs, scores, thr, cut, s_new, k_new, v_new, cache_kt, cache_vt)


def _permute_heads(a, axis):
    shape = a.shape
    a = a.reshape(shape[:axis] + (A_HEADS, HEAD_DIM) + shape[axis + 1:])
    a = jnp.take(a, jnp.array(HEAD_PERM), axis=axis)
    return a.reshape(shape)


def _pack_w_in(w):
    d = w.shape[0]
    o = [0]
    for wd in (D_ATTN, D_KV, D_KV, D_ATTN, IDX_HEADS * IDX_DIM, IDX_HEADS, IDX_DIM,
               D_H, D_H, D_H, D_H):
        o.append(o[-1] + wd)
    a_q, a_k, a_v, a_g, i_q, i_w, i_k, h_q, h_f, h_i, h_g = (
        w[:, o[n]:o[n + 1]] for n in range(11))
    z = lambda n: jnp.zeros((d, n), w.dtype)
    packed = jnp.concatenate(
        [_permute_heads(a_q, 1), a_k, a_v, _permute_heads(a_g, 1), i_q, i_k, i_k,
         z(IDX_DIM), i_w, z(LANES - IDX_DIM - IDX_HEADS), h_q, h_f, h_i, h_g], axis=1)
    assert packed.shape[1] == C_END
    return packed.astype(MXU_DTYPE)


def kernel(x_prompt, x_sample, cache_k, cache_v, cache_ik, state_hgrn, page_table, rel_bias,
           ln_g, w_in, q_norm_g, k_norm_g, hgrn_lb, hgrn_norm_g, w_out):
    b, s, d = x_prompt.shape
    db, dt, _ = x_sample.shape
    depth, n_pool = cache_k.shape[:2]
    n_pages = page_table.shape[1]
    past = n_pages * PAGE
    assert depth == 1 and dt == 1 and hgrn_lb.shape[0] == 2
    assert s % HG_BLK == 0 and s % FAR_CHUNK == 0 and s >= 4 * TOPK and past >= 4 * TOPK
    pps = min(MAX_PAGES_PER_STEP, n_pages)
    assert n_pages % pps == 0 and past % K_CHUNK == 0 and db % 8 == 0
    assert N_BUCKETS // 2 + int(math.log((PAGE + 1) / (N_BUCKETS // 2))
                                / math.log(MAX_DISTANCE / (N_BUCKETS // 2))
                                * (N_BUCKETS - N_BUCKETS // 2)) >= N_BUCKETS - 1

    w_pack = _pack_w_in(w_in[0])
    w_out_pack = jnp.concatenate(
        [_permute_heads(w_out[0][:D_ATTN], 0), w_out[0][D_ATTN:]], axis=0).astype(MXU_DTYPE)
    qg = jnp.tile(q_norm_g[0], A_HEADS)[None]
    kg = jnp.tile(k_norm_g[0], A_KV_HEADS)[None]
    hng = jnp.tile(hgrn_norm_g[0], H_HEADS)[None]
    bd = _block_diag(D_ATTN, HEAD_DIM, 1.0 / HEAD_DIM, MXU_DTYPE)
    lng = ln_g[0][None]

    pp = _project(x_prompt.reshape(b * s, d), lng, w_pack, qg, kg, hgrn_lb, bd, 256)
    r3 = lambda a: a.reshape(b, s, a.shape[-1])
    lp = -(-(s + Q_BLK) // K_CHUNK) * K_CHUNK
    padk = lambda a: jnp.pad(r3(a), ((0, 0), (Q_BLK, lp - s - Q_BLK), (0, 0)))
    vt = padk(pp["vb"]).transpose(0, 2, 1)
    ones = jnp.ones((b, V_ROWS - LANES, lp), vt.dtype)
    vtp = jnp.concatenate([vt[:, :LANES], ones, vt[:, LANES:], ones], axis=1)
    a_out = _attn_prompt(rel_bias, _bias_tiles(rel_bias), pp["qt"], pp["iqt"], pp["iwt"],
                         padk(pp["kb"]), vtp, padk(pp["ik2"]), b, s)
    h_out, st = _hgrn_prompt(pp["hq"], pp["hg"], pp["hk"], pp["hv"], b, s)
    y_prompt = _merge(x_prompt.reshape(b * s, d), a_out.reshape(b * s, D_ATTN), pp["sga"],
                      h_out.reshape(b * s, D_H), pp["sgh"], hng, bd, w_out_pack, 256)
    st = st.reshape(b, D_H // LANES, 2, H_KEY, 2, H_KEY)
    s_prompt = jnp.stack([st[:, :, e, :, e, :] for e in range(2)], axis=2)
    s_prompt = s_prompt.reshape(b, H_HEADS, H_KEY, H_KEY).transpose(0, 1, 3, 2)

    sp = _project(x_sample.reshape(db, d), lng, w_pack, qg, kg, hgrn_lb, bd, db)
    ik_s = sp["ikw"][:, :IDX_DIM]
    iw_s = sp["ikw"][:, IDX_DIM:IDX_DIM + IDX_HEADS]
    scores, s_new = _sample_scores(
        page_table, sp["iqt"].T.reshape(db, IDX_HEADS, IDX_DIM), iw_s.reshape(db, IDX_HEADS, 1),
        ik_s.reshape(db, 1, IDX_DIM), cache_ik[0].transpose(0, 2, 1),
        min(2 * MAX_PAGES_PER_STEP, n_pages))
    thr, cut = _sample_threshold(scores.reshape(db, past).T, s_new[:, :, 0].T)
    lane_b = lambda a: jnp.broadcast_to(a.reshape(db, 1, 1), (db, 1, LANES))
    q_s = sp["qt"].T
    qg4 = q_s.reshape(db, 2, 2, 2, HEAD_DIM)
    rows = []
    for p in range(2):
        g0lo, g0hi = qg4[:, p, 0, 0], qg4[:, p, 0, 1]
        g1lo, g1hi = qg4[:, p, 1, 0], qg4[:, p, 1, 1]
        z1 = jnp.zeros_like(g0lo)
        pr = jnp.stack([jnp.concatenate([g0lo, z1], -1), jnp.concatenate([g1lo, z1], -1),
                        jnp.concatenate([z1, g0hi], -1), jnp.concatenate([z1, g1hi], -1)]
                       + [jnp.zeros((db, LANES), q_s.dtype)] * 4, axis=1)
        rows.append(pr)
    zq = jnp.zeros_like(rows[0])
    qs = jnp.concatenate([jnp.concatenate([rows[0], zq], axis=-1),
                          jnp.concatenate([zq, rows[1]], axis=-1)], axis=1)
    tab_b = jnp.broadcast_to(rel_bias[:, :, None] * LOG2E, (N_BUCKETS, A_HEADS, LANES))
    page_t = lambda c: c[0].transpose(0, 2, 3, 1).reshape(n_pool, D_KV, PAGE)
    a_out_s = _sample_attention(
        page_table, tab_b, qs, scores, lane_b(thr), lane_b(cut), s_new,
        sp["kf"].reshape(db, 1, D_KV), sp["vf"].reshape(db, 1, D_KV),
        page_t(cache_k), page_t(cache_v), pps)
    unpair = lambda a: a.transpose(1, 0, 2).reshape(db, D_H)
    h_out_s, s_sample = _hgrn_sample(unpair(sp["hq"]), unpair(sp["hg"]), unpair(sp["hk"]),
                                     unpair(sp["hv"]),
                                     state_hgrn[0].transpose(1, 2, 3, 0))
    s_sample = s_sample.transpose(3, 0, 1, 2)
    y_sample = _merge(x_sample.reshape(db, d), a_out_s.reshape(db, D_ATTN), sp["sga"],
                      h_out_s.reshape(db, D_H), sp["sgh"], hng, bd, w_out_pack, db)

    kv5 = lambda a, n: a.reshape(1, n, -1, A_KV_HEADS, HEAD_DIM)
    return (y_prompt.reshape(b, s, d), y_sample.reshape(db, 1, d),
            kv5(pp["kf"], b), kv5(pp["vf"], b),
            pp["ikw"][:, :IDX_DIM].reshape(1, b, s, IDX_DIM), s_prompt[None],
            kv5(sp["kf"], db), kv5(sp["vf"], db),
            ik_s.reshape(1, db, 1, IDX_DIM), s_sample[None])
```

```python
import functools
import math

import jax
import jax.numpy as jnp
from jax import lax
from jax.experimental import pallas as pl
from jax.experimental.pallas import tpu as pltpu

F32 = jnp.float32
BF16 = jnp.bfloat16
MXU_DTYPE = BF16

HEAD_DIM = 64
A_HEADS = 8
A_KV_HEADS = 4
D_ATTN = A_HEADS * HEAD_DIM
D_KV = A_KV_HEADS * HEAD_DIM
IDX_HEADS = 8
IDX_DIM = 64
H_HEADS = 8
H_KEY = 64
D_H = H_HEADS * H_KEY
TOPK = 256
PAGE = 128
N_BUCKETS = 32
MAX_DISTANCE = 128
EPS = 1e-6
LANES = 128
Q_BLK = 256
K_CHUNK = 512
FAR_CHUNK = 1024
HG_BLK = 256
HG_CHUNK = 16
V_ROWS = LANES + 16
MAX_PAGES_PER_STEP = 32
VMEM_LIMIT = 56 * 1024 * 1024
NEG_INF = float("-inf")
LOG2E = math.log2(math.e)
KEY_NEG_INF = -2139095041

C_AQ, C_AK, C_AV, C_AG, C_IQ, C_IK, C_IW, C_HQ, C_HF, C_HI, C_HG, C_END = (
    0, 512, 768, 1024, 1536, 2048, 2176, 2304, 2816, 3328, 3840, 4352)
HEAD_PERM = (0, 2, 1, 3, 4, 6, 5, 7)


def _mxu(x):
    return x.astype(MXU_DTYPE)


def _dot(a, b):
    return jnp.dot(a, b, preferred_element_type=F32)


def _dot_nt(a, b):
    return lax.dot_general(a, b, (((1,), (1,)), ((), ())), preferred_element_type=F32)


def _dot_tn(a, b):
    return lax.dot_general(a, b, (((0,), (0,)), ((), ())), preferred_element_type=F32)


def _split_dot(x, w, parts):
    acc = None
    r = x
    for _ in range(parts):
        p = _mxu(r)
        t = _dot(p, w)
        acc = t if acc is None else acc + t
        r = r - p.astype(F32)
    return acc


def _sigmoid(x):
    return 1.0 / (1.0 + jnp.exp(-x))


def _silu(x):
    return x * _sigmoid(x)


def _t5_bucket(n):
    n = jnp.maximum(n, 0)
    max_exact = N_BUCKETS // 2
    nf = jnp.maximum(n, 1).astype(F32)
    large = max_exact + jnp.floor(jnp.log(nf / max_exact) / math.log(MAX_DISTANCE / max_exact)
                                  * (N_BUCKETS - max_exact)).astype(jnp.int32)
    large = jnp.minimum(large, N_BUCKETS - 1)
    return jnp.where(n < max_exact, n, large)


def _block_diag(n, blk, val, dtype):
    r = lax.broadcasted_iota(jnp.int32, (n, n), 0) // blk
    c = lax.broadcasted_iota(jnp.int32, (n, n), 1) // blk
    return jnp.where(r == c, val, 0.0).astype(dtype)


def _proj_kernel(x_ref, lng_ref, w_ref, qg_ref, kg_ref, lb_ref, bd_ref,
                 qt_ref, iqt_ref, iwt_ref, kf_ref, kb_ref, vf_ref, vb_ref, ikw_ref, ik2_ref,
                 sga_ref, hq_ref, hg_ref, hk_ref, hv_ref, sgh_ref):
    x = x_ref[...]
    ms = jnp.mean(x * x, axis=-1, keepdims=True)
    xb = _mxu(x * lax.rsqrt(ms + EPS) * lng_ref[...])

    def seg(a, b):
        return _dot(xb, w_ref[:, a:b])

    bd = bd_ref[...]

    aq = seg(C_AQ, C_AK)
    msq = _split_dot(aq * aq, bd, 2)
    qt_ref[...] = (aq * lax.rsqrt(msq + EPS) * qg_ref[...]
                   * (HEAD_DIM ** -0.5 * LOG2E)).T.astype(qt_ref.dtype)

    ak = seg(C_AK, C_AV)
    msk = _split_dot(ak * ak, bd[:D_KV, :D_KV], 2)
    k = ak * lax.rsqrt(msk + EPS) * kg_ref[...]
    kf_ref[...] = k
    kb_ref[...] = k.astype(kb_ref.dtype)

    v = seg(C_AV, C_AG)
    vf_ref[...] = v
    vb_ref[...] = v.astype(vb_ref.dtype)

    sga_ref[...] = _silu(seg(C_AG, C_IQ))
    iqt_ref[...] = seg(C_IQ, C_IK).T.astype(iqt_ref.dtype)

    ikk = seg(C_IK, C_IW)
    iww = seg(C_IW, C_HQ) * (IDX_HEADS ** -0.5 * IDX_DIM ** -0.5)
    ik2_ref[...] = ikk.astype(ik2_ref.dtype)
    iwt_ref[...] = iww.T[IDX_DIM:IDX_DIM + IDX_HEADS]
    lane = lax.broadcasted_iota(jnp.int32, ikk.shape, 1)
    ikw_ref[...] = jnp.where(lane < IDX_DIM, ikk, iww)

    def put_pairs(ref, val):
        for p in range(D_H // LANES):
            ref[p] = val[:, p * LANES:(p + 1) * LANES]

    put_pairs(hq_ref, _silu(seg(C_HQ, C_HF)) * H_KEY ** -0.5)
    lbp = lb_ref[...]
    mx = jnp.max(lbp, axis=0, keepdims=True)
    e = jnp.exp(lbp - mx)
    lb = e[0:1] / jnp.sum(e, axis=0, keepdims=True)
    f = lb + (1.0 - lb) * _sigmoid(seg(C_HF, C_HI))
    put_pairs(hg_ref, jnp.log(f))
    put_pairs(hk_ref, 1.0 - f)
    put_pairs(hv_ref, seg(C_HI, C_HG))
    sgh_ref[...] = _silu(seg(C_HG, C_END))


def _project(x2d, ln_g, w_pack, qg, kg, lbp, bd, tm):
    n, d = x2d.shape
    row = lambda w: pl.BlockSpec((tm, w), lambda i: (i, 0))
    full = lambda a: pl.BlockSpec(a.shape, lambda i: (0,) * a.ndim)
    outs = [("qt", -D_ATTN, MXU_DTYPE), ("iqt", -D_ATTN, MXU_DTYPE), ("iwt", -IDX_HEADS, F32),
            ("kf", D_KV, F32),
            ("kb", D_KV, MXU_DTYPE), ("vf", D_KV, F32), ("vb", D_KV, MXU_DTYPE),
            ("ikw", LANES, F32), ("ik2", LANES, MXU_DTYPE), ("sga", D_ATTN, F32),
            ("hq", 0, F32), ("hg", 0, F32), ("hk", 0, F32), ("hv", 0, F32),
            ("sgh", D_H, F32)]
    n_pairs = D_H // LANES

    def spec(w):
        if w > 0:
            return row(w), (n, w)
        if w < 0:
            return pl.BlockSpec((-w, tm), lambda i: (0, i)), (-w, n)
        return pl.BlockSpec((n_pairs, tm, LANES), lambda i: (0, i, 0)), (n_pairs, n, LANES)

    res = pl.pallas_call(
        _proj_kernel,
        grid=(n // tm,),
        in_specs=[row(d), full(ln_g), full(w_pack), full(qg), full(kg), full(lbp), full(bd)],
        out_specs=[spec(w)[0] for _, w, _ in outs],
        out_shape=[jax.ShapeDtypeStruct(spec(w)[1], dt) for _, w, dt in outs],
        compiler_params=pltpu.CompilerParams(
            dimension_semantics=("arbitrary",), vmem_limit_bytes=VMEM_LIMIT),
        name="proj",
    )(x2d, ln_g, w_pack, qg, kg, lbp, bd)
    return dict(zip([o[0] for o in outs], res))


def _bias_tile_kernel(tab_ref, o_ref):
    j = lax.broadcasted_iota(jnp.int32, (2 * Q_BLK, Q_BLK), 0)
    r = lax.broadcasted_iota(jnp.int32, (2 * Q_BLK, Q_BLK), 1)
    bucket = _t5_bucket(Q_BLK + r - j)
    for h in range(A_HEADS):
        acc = jnp.zeros((2 * Q_BLK, Q_BLK), F32)
        for b in range(N_BUCKETS):
            acc = jnp.where(bucket == b, tab_ref[b, h] * LOG2E, acc)
        o_ref[h] = acc


def _bias_tiles(rel_bias):
    return pl.pallas_call(
        _bias_tile_kernel,
        in_specs=[pl.BlockSpec(memory_space=pltpu.SMEM)],
        out_shape=jax.ShapeDtypeStruct((A_HEADS, 2 * Q_BLK, Q_BLK), F32),
        name="bias_tiles",
    )(rel_bias)


def _key_to_f32(key):
    bits = key ^ ((key >> 31) & jnp.int32(0x7FFFFFFF))
    return lax.bitcast_convert_type(bits, F32)


def _topk_cut(sc_ref, nch, cut_ref, tie_ref, extra=None):
    ncol = sc_ref.shape[1]
    nrows = nch * K_CHUNK
    nbits = int(sc_ref.shape[0]).bit_length()
    groups = K_CHUNK // 8

    def count(pred, ref, extra_val, side=None):
        def body(c, acc):
            r0 = pl.multiple_of(c * K_CHUNK, K_CHUNK)
            v = ref[pl.ds(r0, K_CHUNK), :]
            if side is not None:
                side(v, r0)
            hit = jnp.where(pred(v), 1, 0).astype(jnp.int32).reshape(8, groups // 8, 8, ncol)
            parts = [jnp.sum(hit[g], axis=0, dtype=jnp.int32) for g in range(8)]
            return acc + (((parts[0] + parts[1]) + (parts[2] + parts[3]))
                          + ((parts[4] + parts[5]) + (parts[6] + parts[7])))

        acc = lax.fori_loop(0, nch, body, jnp.zeros((8, ncol), jnp.int32))
        cnt = jnp.sum(acc, axis=0, keepdims=True, dtype=jnp.int32)
        if extra_val is not None:
            cnt = cnt + jnp.where(pred(extra_val), 1, 0).astype(jnp.int32)
        return cnt

    def bit_body(it, carry):
        key, n_ge = carry
        cand = key + jnp.left_shift(jnp.int32(1), 31 - it)
        cf = _key_to_f32(cand)
        cnt = count(lambda v: v >= cf, sc_ref, extra)
        ok = cnt >= TOPK
        return jnp.where(ok, cand, key), jnp.where(ok, cnt, n_ge)

    key0 = jnp.full((1, ncol), jnp.iinfo(jnp.int32).min, jnp.int32)
    key, n_ge = lax.fori_loop(0, 32, bit_body, (key0, jnp.full((1, ncol), TOPK, jnp.int32)))
    below = key < KEY_NEG_INF
    thr = _key_to_f32(jnp.maximum(key, KEY_NEG_INF))
    cut_ref[...] = jnp.full((1, ncol), jnp.iinfo(jnp.int32).max, jnp.int32)

    @pl.when(jnp.max(jnp.where(below, TOPK + 1, n_ge)) > TOPK)
    def _():
        big = jnp.int32(1 << 30)

        def mark(v, r0):
            row = r0 + lax.broadcasted_iota(jnp.int32, v.shape, 0)
            tie_ref[pl.ds(r0, K_CHUNK), :] = jnp.where(v == thr, row, big)

        need = TOPK - count(lambda v: v > thr, sc_ref, extra, side=mark)
        extra_tie = None if extra is None else jnp.where(extra == thr, nrows, big)

        def idx_body(it, x):
            cand = x + jnp.left_shift(jnp.int32(1), nbits - 1 - it)
            g = count(lambda t: t < cand, tie_ref, extra_tie)
            return jnp.where(g < need, cand, x)

        cut_ref[...] = lax.fori_loop(0, nbits, idx_body, jnp.zeros((1, ncol), jnp.int32))

    return thr


def _in_topk(sc, col, thr, cut):
    return (sc > thr) | ((sc == thr) & (col <= cut))


def _row_halves(x):
    half = x.shape[0] // 2
    zero = jnp.zeros((half, x.shape[1]), x.dtype)
    return (jnp.concatenate([x[:half], zero], axis=0), jnp.concatenate([zero, x[half:]], axis=0))


def _attn_prompt_kernel(tab_ref, qt_ref, iqt_ref, iwt_ref, kb_ref, vt_ref, ik2_ref, bt_ref,
                        o_ref, sc_ref, tie_ref, cut_ref, m_ref, acc_ref):
    i = pl.program_id(1)
    t_pos = i * Q_BLK + lax.broadcasted_iota(jnp.int32, (1, Q_BLK), 1)

    @pl.when((pl.program_id(0) == 0) & (i == 0))
    def _():
        sc_ref[...] = jnp.full(sc_ref.shape, NEG_INF, F32)

    iqt = iqt_ref[...]
    cols = []
    for g in range(D_ATTN // LANES):
        cols.extend(_row_halves(iqt[g * LANES:(g + 1) * LANES]))
    iq_stack = jnp.concatenate(cols, axis=1)
    iw = iwt_ref[...]
    nch = ((i + 2) * Q_BLK + K_CHUNK - 1) // K_CHUNK

    def score_body(c, carry):
        r0 = pl.multiple_of(c * K_CHUNK, K_CHUNK)
        s_all = _dot(ik2_ref[0, pl.ds(r0, K_CHUNK), :], iq_stack)
        sc = jnp.zeros((K_CHUNK, Q_BLK), F32)
        for h in range(IDX_HEADS):
            sc = sc + jnp.maximum(s_all[:, h * Q_BLK:(h + 1) * Q_BLK], 0.0) * iw[h:h + 1]
        s_glob = r0 - Q_BLK + lax.broadcasted_iota(jnp.int32, (K_CHUNK, Q_BLK), 0)
        valid = (s_glob >= 0) & (s_glob <= t_pos)
        sc_ref[pl.ds(r0, K_CHUNK), :] = jnp.where(valid, sc, NEG_INF)
        return carry

    lax.fori_loop(0, nch, score_body, 0)
    thr = _topk_cut(sc_ref, nch, cut_ref, tie_ref)
    cut = cut_ref[...]

    m_ref[...] = jnp.full(m_ref.shape, NEG_INF, F32)
    acc_ref[...] = jnp.zeros(acc_ref.shape, F32)

    qt = qt_ref[...]
    q_pairs = []
    for p in range(2):
        g0 = _row_halves(qt[(2 * p) * LANES:(2 * p + 1) * LANES])
        g1 = _row_halves(qt[(2 * p + 1) * LANES:(2 * p + 2) * LANES])
        q_pairs.append(jnp.concatenate([g0[0], g1[0], g0[1], g1[1]], axis=1))

    def attend(r0, width, valid, bias_fn, const_fn):
        madd = jnp.where(valid, 0.0, NEG_INF)
        for p in range(2):
            kc = kb_ref[0, pl.ds(r0, width), p * LANES:(p + 1) * LANES]
            vc = vt_ref[0, p * V_ROWS:(p + 1) * V_ROWS, pl.ds(r0, width)]
            st = _dot(kc, q_pairs[p])
            ps, alphas = [], []
            for a in range(4):
                h = 4 * p + a
                x = st[:, a * Q_BLK:(a + 1) * Q_BLK] + madd
                bias = bias_fn(h)
                if bias is not None:
                    x = x + bias
                c = const_fn(h)
                m_old = m_ref[h:h + 1]
                m_new = jnp.maximum(m_old, jnp.max(x, axis=0, keepdims=True) + c)
                m_safe = jnp.where(m_new == NEG_INF, 0.0, m_new)
                m_ref[h:h + 1] = m_new
                ps.append(_mxu(jnp.exp2(x - (m_safe - c))))
                alphas.append(jnp.exp2(m_old - m_safe))
            pv = _dot(vc, jnp.concatenate(ps, axis=1))
            acc_ref[p] = acc_ref[p] * jnp.concatenate(alphas, axis=1) + pv

    def far_body(c, carry):
        r0 = pl.multiple_of(c * FAR_CHUNK, FAR_CHUNK)
        row = r0 + lax.broadcasted_iota(jnp.int32, (FAR_CHUNK, Q_BLK), 0)
        valid = ((row >= Q_BLK) & (row < i * Q_BLK)
                 & _in_topk(sc_ref[pl.ds(r0, FAR_CHUNK), :], row, thr, cut))
        attend(r0, FAR_CHUNK, valid, lambda h: None, lambda h: tab_ref[N_BUCKETS - 1, h] * LOG2E)
        return carry

    lax.fori_loop(0, (i * Q_BLK + FAR_CHUNK - 1) // FAR_CHUNK, far_body, 0)

    r0 = pl.multiple_of(i * Q_BLK, Q_BLK)
    row = r0 + lax.broadcasted_iota(jnp.int32, (2 * Q_BLK, Q_BLK), 0)
    valid = ((row >= Q_BLK) & (row - Q_BLK <= t_pos)
             & _in_topk(sc_ref[pl.ds(r0, 2 * Q_BLK), :], row, thr, cut))
    attend(r0, 2 * Q_BLK, valid, lambda h: bt_ref[h], lambda h: 0.0)

    top = lax.broadcasted_iota(jnp.int32, (LANES, Q_BLK), 0) < HEAD_DIM
    for p in range(2):
        acc = acc_ref[p]
        outs = [acc[:LANES, a * Q_BLK:(a + 1) * Q_BLK] / acc[LANES:LANES + 1, a * Q_BLK:(a + 1) * Q_BLK]
                for a in range(4)]
        o_ref[0, :, (2 * p) * LANES:(2 * p + 1) * LANES] = jnp.where(top, outs[0], outs[2]).T
        o_ref[0, :, (2 * p + 1) * LANES:(2 * p + 2) * LANES] = jnp.where(top, outs[1], outs[3]).T


def _attn_prompt(rel_bias, bias_tiles, qt, iqt, iwt, kbp, vtp, ik2p, b, s):
    nq = s // Q_BLK
    lp = kbp.shape[1]
    blk_t = lambda r: pl.BlockSpec((r, Q_BLK), lambda bi, i: (0, bi * nq + i))
    return pl.pallas_call(
        _attn_prompt_kernel,
        grid=(b, s // Q_BLK),
        in_specs=[pl.BlockSpec(memory_space=pltpu.SMEM),
                  blk_t(D_ATTN), blk_t(D_ATTN), blk_t(IDX_HEADS),
                  pl.BlockSpec((1, lp, D_KV), lambda bi, i: (bi, 0, 0)),
                  pl.BlockSpec((1, 2 * V_ROWS, lp), lambda bi, i: (bi, 0, 0)),
                  pl.BlockSpec((1, lp, LANES), lambda bi, i: (bi, 0, 0)),
                  pl.BlockSpec(bias_tiles.shape, lambda bi, i: (0, 0, 0))],
        out_specs=pl.BlockSpec((1, Q_BLK, D_ATTN), lambda bi, i: (bi, i, 0)),
        out_shape=jax.ShapeDtypeStruct((b, s, D_ATTN), F32),
        scratch_shapes=[pltpu.VMEM((lp, Q_BLK), F32),
                        pltpu.VMEM((lp, Q_BLK), jnp.int32),
                        pltpu.VMEM((1, Q_BLK), jnp.int32),
                        pltpu.VMEM((A_HEADS, Q_BLK), F32),
                        pltpu.VMEM((2, V_ROWS, 4 * Q_BLK), F32)],
        compiler_params=pltpu.CompilerParams(
            dimension_semantics=("arbitrary", "arbitrary"), vmem_limit_bytes=VMEM_LIMIT),
        name="attn_prompt",
    )(rel_bias, qt, iqt, iwt, kbp, vtp, ik2p, bias_tiles)


def _hgrn_prompt_kernel(q_ref, g_ref, k_ref, v_ref, o_ref, st_ref, b_ref, oi_ref,
                        tq_ref, tk_ref, tv_ref, tb_ref):
    j = pl.program_id(1)
    n_pairs = D_H // LANES
    n_chunks = HG_BLK // HG_CHUNK

    @pl.when(j == 0)
    def _():
        st_ref[...] = jnp.zeros(st_ref.shape, F32)

    def lanes(ref, rows):
        return jnp.concatenate([ref[p, rows, :] for p in range(n_pairs)], axis=1)

    r = lax.broadcasted_iota(jnp.int32, (HG_BLK, HG_BLK), 0)
    c = lax.broadcasted_iota(jnp.int32, (HG_BLK, HG_BLK), 1)
    tri = jnp.where(c <= r, 1.0, 0.0).astype(MXU_DTYPE)
    bcum = _split_dot_left(tri, lanes(g_ref, slice(None)), 3)
    for p in range(n_pairs):
        b_ref[p] = bcum[:, p * LANES:(p + 1) * LANES]

    for t in range(HG_CHUNK):
        rows = pl.ds(t, n_chunks, stride=HG_CHUNK)
        tq_ref[t] = lanes(q_ref, rows)
        tk_ref[t] = lanes(k_ref, rows)
        tv_ref[t] = lanes(v_ref, rows)
        tb_ref[t] = lanes(b_ref, rows)

    half = D_H // 2
    bd_half = _block_diag(half, H_KEY, 1.0, MXU_DTYPE)
    for t in range(HG_CHUNK):
        qt = tq_ref[t]
        bt = tb_ref[t]
        parts = [qt * tk_ref[s] * jnp.exp(bt - tb_ref[s]) for s in range(t)]
        parts.append(qt * tk_ref[t])
        pr = _mxu(jnp.concatenate(parts, axis=0))
        rs = jnp.concatenate([_dot(pr[:, :half], bd_half), _dot(pr[:, half:], bd_half)], axis=1)
        acc = rs[0:n_chunks] * tv_ref[0]
        for s in range(1, t + 1):
            acc = acc + rs[s * n_chunks:(s + 1) * n_chunks] * tv_ref[s]
        for p in range(n_pairs):
            oi_ref[p, pl.ds(t, n_chunks, stride=HG_CHUNK), :] = acc[:, p * LANES:(p + 1) * LANES]

    lane_r = lax.broadcasted_iota(jnp.int32, (LANES, LANES), 0) // H_KEY
    lane_c = lax.broadcasted_iota(jnp.int32, (LANES, LANES), 1) // H_KEY
    same_head = lane_r == lane_c
    for cch in range(n_chunks):
        rows = slice(cch * HG_CHUNK, (cch + 1) * HG_CHUNK)
        last = slice((cch + 1) * HG_CHUNK - 1, (cch + 1) * HG_CHUNK)
        prev = slice(cch * HG_CHUNK - 1, cch * HG_CHUNK)
        outs = []
        for p in range(n_pairs):
            bch = b_ref[p, rows, :]
            b0 = b_ref[p, prev, :] if cch > 0 else jnp.zeros((1, LANES), F32)
            bl = b_ref[p, last, :]
            qi = _mxu(q_ref[p, rows, :] * jnp.exp(bch - b0))
            kl = _mxu(k_ref[p, rows, :] * jnp.exp(bl - bch))
            st = st_ref[0, p]
            outs.append(oi_ref[p, rows, :] + _dot_nt(qi, _mxu(st)))
            ds = _dot_tn(_mxu(v_ref[p, rows, :]), kl)
            st_ref[0, p] = st * jnp.exp(bl - b0) + jnp.where(same_head, ds, 0.0)
        o_ref[0, rows, :] = jnp.concatenate(outs, axis=1)


def _split_dot_left(w, x, parts):
    acc = None
    r = x
    for _ in range(parts):
        p = _mxu(r)
        t = _dot(w, p)
        acc = t if acc is None else acc + t
        r = r - p.astype(F32)
    return acc


def _hgrn_prompt(hq, hg, hk, hv, b, s):
    n_pairs = D_H // LANES
    nblk = s // HG_BLK
    blk = pl.BlockSpec((n_pairs, HG_BLK, LANES), lambda bi, j: (0, bi * nblk + j, 0))
    return pl.pallas_call(
        _hgrn_prompt_kernel,
        grid=(b, nblk),
        in_specs=[blk, blk, blk, blk],
        out_specs=[pl.BlockSpec((1, HG_BLK, D_H), lambda bi, j: (bi, j, 0)),
                   pl.BlockSpec((1, n_pairs, LANES, LANES), lambda bi, j: (bi, 0, 0, 0))],
        out_shape=[jax.ShapeDtypeStruct((b, s, D_H), F32),
                   jax.ShapeDtypeStruct((b, n_pairs, LANES, LANES), F32)],
        scratch_shapes=[pltpu.VMEM((n_pairs, HG_BLK, LANES), F32)] * 2
        + [pltpu.VMEM((HG_CHUNK, HG_BLK // HG_CHUNK, D_H), F32)] * 4,
        compiler_params=pltpu.CompilerParams(
            dimension_semantics=("arbitrary", "arbitrary"), vmem_limit_bytes=VMEM_LIMIT),
        name="hgrn_prompt",
    )(hq, hg, hk, hv)


def _hgrn_sample_kernel(q_ref, g_ref, k_ref, v_ref, s_ref, o_ref, sn_ref):
    q = q_ref[...]
    e = jnp.exp(g_ref[...])
    k = k_ref[...]
    v = v_ref[...]
    o = jnp.zeros(v.shape, F32)
    for kk in range(H_KEY):
        sn = e[kk:kk + 1] * s_ref[0, kk] + k[kk:kk + 1] * v
        sn_ref[0, kk] = sn
        o = o + q[kk:kk + 1] * sn
    o_ref[...] = o


def _hgrn_sample(hq, hg, hk, hv, state_t):
    db = hq.shape[0]
    hspec = pl.BlockSpec((H_KEY, db), lambda h: (h, 0))
    sspec = pl.BlockSpec((1, H_KEY, H_KEY, db), lambda h: (h, 0, 0, 0))
    o_t, s_new = pl.pallas_call(
        _hgrn_sample_kernel,
        grid=(H_HEADS,),
        in_specs=[hspec, hspec, hspec, hspec, sspec],
        out_specs=[hspec, sspec],
        out_shape=[jax.ShapeDtypeStruct((D_H, db), F32),
                   jax.ShapeDtypeStruct(state_t.shape, F32)],
        compiler_params=pltpu.CompilerParams(
            dimension_semantics=("arbitrary",), vmem_limit_bytes=VMEM_LIMIT),
        name="hgrn_sample",
    )(hq.T, hg.T, hk.T, hv.T, state_t)
    return o_t.T, s_new


def _merge_kernel(x_ref, a_ref, sga_ref, h_ref, sgh_ref, hng_ref, bd_ref, w_ref, y_ref):
    h = h_ref[...]
    msq = _split_dot(h * h, bd_ref[...], 2)
    hn = h * lax.rsqrt(msq + EPS) * hng_ref[...]
    ma = _mxu(a_ref[...] * sga_ref[...])
    mh = _mxu(hn * sgh_ref[...])
    y_ref[...] = x_ref[...] + _dot(ma, w_ref[0:D_ATTN, :]) + _dot(mh, w_ref[D_ATTN:, :])


def _merge(x2d, a_out, sga, h_out, sgh, hng, bd, w_pack, tm):
    n, d = x2d.shape
    row = lambda w: pl.BlockSpec((tm, w), lambda i: (i, 0))
    full = lambda a: pl.BlockSpec(a.shape, lambda i: (0,) * a.ndim)
    return pl.pallas_call(
        _merge_kernel,
        grid=(n // tm,),
        in_specs=[row(d), row(D_ATTN), row(D_ATTN), row(D_H), row(D_H),
                  full(hng), full(bd), full(w_pack)],
        out_specs=row(d),
        out_shape=jax.ShapeDtypeStruct((n, d), F32),
        compiler_params=pltpu.CompilerParams(
            dimension_semantics=("arbitrary",), vmem_limit_bytes=VMEM_LIMIT),
        name="merge",
    )(x2d, a_out, sga, h_out, sgh, hng, bd, w_pack)


def _sample_score_kernel(pps, pt_ref, iqh_ref, iwc_ref, ikn_ref, cik_ref, sc_ref, sn_ref,
                         ikbuf, sem):
    b = pl.program_id(0)
    j = pl.program_id(1)
    nj = pl.num_programs(1)
    step = b * nj + j
    slot = step % 2

    def page_copies(bb, jj, sl):
        return [pltpu.make_async_copy(cik_ref.at[pt_ref[bb, jj * pps + r]], ikbuf.at[sl, r],
                                      sem.at[sl]) for r in range(pps)]

    @pl.when(step == 0)
    def _():
        for n_cp, cp in enumerate(page_copies(0, 0, 0)):
            cp.start(priority=n_cp % 2)

    @pl.when(step + 1 < pl.num_programs(0) * nj)
    def _():
        nxt = step + 1
        for n_cp, cp in enumerate(page_copies(nxt // nj, nxt % nj, 1 - slot)):
            cp.start(priority=n_cp % 2)

    for cp in page_copies(b, j, slot):
        cp.wait()
    iqh = iqh_ref[0]
    iwc = iwc_ref[0]
    ikt = jnp.concatenate([_mxu(ikbuf[slot, r]) for r in range(pps)], axis=1)
    s = _dot(iqh, ikt)
    sc_ref[0] = jnp.sum(jnp.maximum(s, 0.0) * iwc, axis=0, keepdims=True)
    prod = iqh.astype(F32) * _mxu(ikn_ref[0]).astype(F32)
    s_new = jnp.sum(prod, axis=1, keepdims=True)
    s_new = jnp.sum(jnp.maximum(s_new, 0.0) * iwc, axis=0, keepdims=True)
    sn_ref[0] = jnp.broadcast_to(s_new, (1, LANES))


def _sample_scores(page_table, iqh, iwc, ikn, cache_ikt, pps):
    db, n_pages = page_table.shape
    gs = pltpu.PrefetchScalarGridSpec(
        num_scalar_prefetch=1,
        grid=(db, n_pages // pps),
        in_specs=[pl.BlockSpec((1, IDX_HEADS, IDX_DIM), lambda b, j, pt: (b, 0, 0)),
                  pl.BlockSpec((1, IDX_HEADS, 1), lambda b, j, pt: (b, 0, 0)),
                  pl.BlockSpec((1, 1, IDX_DIM), lambda b, j, pt: (b, 0, 0)),
                  pl.BlockSpec(memory_space=pl.ANY)],
        out_specs=[pl.BlockSpec((1, 1, pps * PAGE), lambda b, j, pt: (b, 0, j)),
                   pl.BlockSpec((1, 1, LANES), lambda b, j, pt: (b, 0, 0))],
        scratch_shapes=[pltpu.VMEM((2, pps, IDX_DIM, PAGE), F32),
                        pltpu.SemaphoreType.DMA((2,))],
    )
    return pl.pallas_call(
        functools.partial(_sample_score_kernel, pps),
        grid_spec=gs,
        out_shape=[jax.ShapeDtypeStruct((db, 1, n_pages * PAGE), F32),
                   jax.ShapeDtypeStruct((db, 1, LANES), F32)],
        compiler_params=pltpu.CompilerParams(
            dimension_semantics=("arbitrary", "arbitrary"), vmem_limit_bytes=VMEM_LIMIT),
        name="sample_scores",
    )(page_table, iqh, iwc, ikn, cache_ikt)


def _sample_thr_kernel(sc_ref, sn_ref, thr_ref, cut_ref, tie_ref):
    past = sc_ref.shape[0]
    thr_ref[...] = _topk_cut(sc_ref, past // K_CHUNK, cut_ref, tie_ref, extra=sn_ref[...])


def _sample_threshold(scores_t, s_new_t):
    db = scores_t.shape[1]
    return pl.pallas_call(
        _sample_thr_kernel,
        out_shape=[jax.ShapeDtypeStruct((1, db), F32),
                   jax.ShapeDtypeStruct((1, db), jnp.int32)],
        scratch_shapes=[pltpu.VMEM(scores_t.shape, jnp.int32)],
        compiler_params=pltpu.CompilerParams(vmem_limit_bytes=VMEM_LIMIT),
        name="sample_threshold",
    )(scores_t, s_new_t)


def _sample_attn_kernel(pps, pt_ref, tb_ref, qs_ref, sc_ref, thr_ref, cut_ref, sn_ref,
                        kn_ref, vn_ref, ck_ref, cv_ref, o_ref,
                        kbuf, vbuf, sem, m_ref, l_ref, acc_ref):
    b = pl.program_id(0)
    j = pl.program_id(1)
    nb = pl.num_programs(0)
    nj = pl.num_programs(1)
    n = pps * PAGE
    past = nj * n
    step = b * nj + j
    slot = step % 2

    def page_copies(bb, jj, sl):
        copies = []
        for r in range(pps):
            page = pt_ref[bb, jj * pps + r]
            copies.append(pltpu.make_async_copy(ck_ref.at[page], kbuf.at[sl, r], sem.at[0, sl]))
            copies.append(pltpu.make_async_copy(cv_ref.at[page], vbuf.at[sl, r], sem.at[1, sl]))
        return copies

    @pl.when(step == 0)
    def _():
        for n_cp, cp in enumerate(page_copies(0, 0, 0)):
            cp.start(priority=(n_cp // 2) % 2)

    @pl.when(step + 1 < nb * nj)
    def _():
        nxt = step + 1
        for n_cp, cp in enumerate(page_copies(nxt // nj, nxt % nj, 1 - slot)):
            cp.start(priority=(n_cp // 2) % 2)

    for cp in page_copies(b, j, slot):
        cp.wait()
    k_refs = [kbuf.at[slot, r] for r in range(pps)]
    v_refs = [vbuf.at[slot, r] for r in range(pps)]

    @pl.when(j == 0)
    def _():
        m_ref[...] = jnp.full(m_ref.shape, NEG_INF, F32)
        l_ref[...] = jnp.zeros(l_ref.shape, F32)
        acc_ref[...] = jnp.zeros(acc_ref.shape, F32)

    thr = thr_ref[0, :, 0:1]
    cut = cut_ref[0, :, 0:1]
    qs = qs_ref[0]
    lo = lax.broadcasted_iota(jnp.int32, (1, LANES), 1) < HEAD_DIM

    def update(x, pv_fn):
        m_old = m_ref[...]
        m_new = jnp.maximum(m_old, jnp.max(x, axis=1, keepdims=True))
        m_safe = jnp.where(m_new == NEG_INF, 0.0, m_new)
        alpha = jnp.exp2(m_old - m_safe)
        pe = jnp.exp2(x - m_safe)
        l_ref[...] = alpha * l_ref[...] + jnp.sum(pe, axis=1, keepdims=True)
        m_ref[...] = m_new
        acc_ref[...] = alpha * acc_ref[...] + pv_fn(_mxu(pe))

    def head_rows(b8):
        z = jnp.zeros((4, b8.shape[1]), F32)
        return jnp.concatenate([b8[0:4], z, b8[4:8], z], axis=0)

    s_idx = j * n + lax.broadcasted_iota(jnp.int32, (1, n), 1)
    s_last = s_idx[:, n - PAGE:]
    bucket = _t5_bucket(past - s_last)
    near = jnp.zeros((A_HEADS, PAGE), F32)
    for b in range(N_BUCKETS):
        near = jnp.where(bucket == b, tb_ref[b], near)
    far = tb_ref[N_BUCKETS - 1]
    bias8 = jnp.concatenate([far] * (pps - 1) + [jnp.where(j == nj - 1, near, far)], axis=1)
    maskadd = jnp.where(_in_topk(sc_ref[0], s_idx, thr, cut), 0.0, NEG_INF)
    kt = jnp.concatenate([_mxu(r[...]) for r in k_refs], axis=1)
    vt = jnp.concatenate([_mxu(r[...]) for r in v_refs], axis=1)
    update(_dot(qs, kt) + head_rows(bias8) + maskadd, lambda pe: _dot_nt(pe, vt))

    @pl.when(j == nj - 1)
    def _():
        valid = (_in_topk(sn_ref[0], past, thr, cut)
                 & (lax.broadcasted_iota(jnp.int32, (1, LANES), 1) == 0))
        madd = jnp.where(valid, 0.0, NEG_INF)
        kn = _mxu(jnp.broadcast_to(kn_ref[0], (LANES, D_KV)))
        vn = _mxu(jnp.broadcast_to(vn_ref[0], (LANES, D_KV)))
        update(_dot_nt(qs, kn) + head_rows(tb_ref[0]) + madd, lambda pe: _dot(pe, vn))
        out = acc_ref[...] / l_ref[...]
        for p in range(2):
            o = out[8 * p:8 * p + 4, p * LANES:(p + 1) * LANES]
            o_ref[0, :, (2 * p) * LANES:(2 * p + 1) * LANES] = jnp.where(lo, o[0:1], o[2:3])
            o_ref[0, :, (2 * p + 1) * LANES:(2 * p + 2) * LANES] = jnp.where(lo, o[1:2], o[3:4])


def _sample_attention(page_table, tab_b, qs, scores, thr, cut, s_new, k_new, v_new,
                      cache_kt, cache_vt, pps):
    db, n_pages = page_table.shape
    per_seq = lambda shape: pl.BlockSpec((1,) + shape, lambda b, j, pt: (b,) + (0,) * len(shape))
    gs = pltpu.PrefetchScalarGridSpec(
        num_scalar_prefetch=1,
        grid=(db, n_pages // pps),
        in_specs=[pl.BlockSpec(tab_b.shape, lambda b, j, pt: (0, 0, 0)),
                  per_seq((2 * A_HEADS, D_KV)),
                  pl.BlockSpec((1, 1, pps * PAGE), lambda b, j, pt: (b, 0, j)),
                  per_seq((1, LANES)), per_seq((1, LANES)), per_seq((1, LANES)),
                  per_seq((1, D_KV)), per_seq((1, D_KV)),
                  pl.BlockSpec(memory_space=pl.ANY), pl.BlockSpec(memory_space=pl.ANY)],
        out_specs=per_seq((1, D_ATTN)),
        scratch_shapes=[pltpu.VMEM((2, pps, D_KV, PAGE), F32),
                        pltpu.VMEM((2, pps, D_KV, PAGE), F32),
                        pltpu.SemaphoreType.DMA((2, 2)),
                        pltpu.VMEM((2 * A_HEADS, 1), F32),
                        pltpu.VMEM((2 * A_HEADS, 1), F32),
                        pltpu.VMEM((2 * A_HEADS, D_KV), F32)],
    )
    return pl.pallas_call(
        functools.partial(_sample_attn_kernel, pps),
        grid_spec=gs,
        out_shape=jax.ShapeDtypeStruct((db, 1, D_ATTN), F32),
        compiler_params=pltpu.CompilerParams(
            dimension_semantics=("arbitrary", "arbitrary"), vmem_limit_bytes=VMEM_LIMIT),
        name="sample_attn",
    )(page_table, tab_b, qs, scores, thr, cut, s_new, k_new, v_new, cache_kt, cache_vt)


def _permute_heads(a, axis):
    shape = a.shape
    a = a.reshape(shape[:axis] + (A_HEADS, HEAD_DIM) + shape[axis + 1:])
    a = jnp.take(a, jnp.array(HEAD_PERM), axis=axis)
    return a.reshape(shape)


def _pack_w_in(w):
    d = w.shape[0]
    o = [0]
    for wd in (D_ATTN, D_KV, D_KV, D_ATTN, IDX_HEADS * IDX_DIM, IDX_HEADS, IDX_DIM,
               D_H, D_H, D_H, D_H):
        o.append(o[-1] + wd)
    a_q, a_k, a_v, a_g, i_q, i_w, i_k, h_q, h_f, h_i, h_g = (
        w[:, o[n]:o[n + 1]] for n in range(11))
    z = lambda n: jnp.zeros((d, n), w.dtype)
    packed = jnp.concatenate(
        [_permute_heads(a_q, 1), a_k, a_v, _permute_heads(a_g, 1), i_q, i_k, i_k,
         z(IDX_DIM), i_w, z(LANES - IDX_DIM - IDX_HEADS), h_q, h_f, h_i, h_g], axis=1)
    assert packed.shape[1] == C_END
    return packed.astype(MXU_DTYPE)


def kernel(x_prompt, x_sample, cache_k, cache_v, cache_ik, state_hgrn, page_table, rel_bias,
           ln_g, w_in, q_norm_g, k_norm_g, hgrn_lb, hgrn_norm_g, w_out):
    b, s, d = x_prompt.shape
    db, dt, _ = x_sample.shape
    depth, n_pool = cache_k.shape[:2]
    n_pages = page_table.shape[1]
    past = n_pages * PAGE
    assert depth == 1 and dt == 1 and hgrn_lb.shape[0] == 2
    assert s % HG_BLK == 0 and s % FAR_CHUNK == 0 and s >= 4 * TOPK and past >= 4 * TOPK
    pps = min(MAX_PAGES_PER_STEP, n_pages)
    assert n_pages % pps == 0 and past % K_CHUNK == 0 and db % 8 == 0
    assert N_BUCKETS // 2 + int(math.log((PAGE + 1) / (N_BUCKETS // 2))
                                / math.log(MAX_DISTANCE / (N_BUCKETS // 2))
                                * (N_BUCKETS - N_BUCKETS // 2)) >= N_BUCKETS - 1

    w_pack = _pack_w_in(w_in[0])
    w_out_pack = jnp.concatenate(
        [_permute_heads(w_out[0][:D_ATTN], 0), w_out[0][D_ATTN:]], axis=0).astype(MXU_DTYPE)
    qg = jnp.tile(q_norm_g[0], A_HEADS)[None]
    kg = jnp.tile(k_norm_g[0], A_KV_HEADS)[None]
    hng = jnp.tile(hgrn_norm_g[0], H_HEADS)[None]
    bd = _block_diag(D_ATTN, HEAD_DIM, 1.0 / HEAD_DIM, MXU_DTYPE)
    lng = ln_g[0][None]

    pp = _project(x_prompt.reshape(b * s, d), lng, w_pack, qg, kg, hgrn_lb, bd, 256)
    r3 = lambda a: a.reshape(b, s, a.shape[-1])
    lp = -(-(s + Q_BLK) // K_CHUNK) * K_CHUNK
    padk = lambda a: jnp.pad(r3(a), ((0, 0), (Q_BLK, lp - s - Q_BLK), (0, 0)))
    vt = padk(pp["vb"]).transpose(0, 2, 1)
    ones = jnp.ones((b, V_ROWS - LANES, lp), vt.dtype)
    vtp = jnp.concatenate([vt[:, :LANES], ones, vt[:, LANES:], ones], axis=1)
    a_out = _attn_prompt(rel_bias, _bias_tiles(rel_bias), pp["qt"], pp["iqt"], pp["iwt"],
                         padk(pp["kb"]), vtp, padk(pp["ik2"]), b, s)
    h_out, st = _hgrn_prompt(pp["hq"], pp["hg"], pp["hk"], pp["hv"], b, s)
    y_prompt = _merge(x_prompt.reshape(b * s, d), a_out.reshape(b * s, D_ATTN), pp["sga"],
                      h_out.reshape(b * s, D_H), pp["sgh"], hng, bd, w_out_pack, 256)
    st = st.reshape(b, D_H // LANES, 2, H_KEY, 2, H_KEY)
    s_prompt = jnp.stack([st[:, :, e, :, e, :] for e in range(2)], axis=2)
    s_prompt = s_prompt.reshape(b, H_HEADS, H_KEY, H_KEY).transpose(0, 1, 3, 2)

    sp = _project(x_sample.reshape(db, d), lng, w_pack, qg, kg, hgrn_lb, bd, db)
    ik_s = sp["ikw"][:, :IDX_DIM]
    iw_s = sp["ikw"][:, IDX_DIM:IDX_DIM + IDX_HEADS]
    scores, s_new = _sample_scores(
        page_table, sp["iqt"].T.reshape(db, IDX_HEADS, IDX_DIM), iw_s.reshape(db, IDX_HEADS, 1),
        ik_s.reshape(db, 1, IDX_DIM), cache_ik[0].transpose(0, 2, 1),
        min(2 * MAX_PAGES_PER_STEP, n_pages))
    thr, cut = _sample_threshold(scores.reshape(db, past).T, s_new[:, :, 0].T)
    lane_b = lambda a: jnp.broadcast_to(a.reshape(db, 1, 1), (db, 1, LANES))
    q_s = sp["qt"].T
    qg4 = q_s.reshape(db, 2, 2, 2, HEAD_DIM)
    rows = []
    for p in range(2):
        g0lo, g0hi = qg4[:, p, 0, 0], qg4[:, p, 0, 1]
        g1lo, g1hi = qg4[:, p, 1, 0], qg4[:, p, 1, 1]
        z1 = jnp.zeros_like(g0lo)
        pr = jnp.stack([jnp.concatenate([g0lo, z1], -1), jnp.concatenate([g1lo, z1], -1),
                        jnp.concatenate([z1, g0hi], -1), jnp.concatenate([z1, g1hi], -1)]
                       + [jnp.zeros((db, LANES), q_s.dtype)] * 4, axis=1)
        rows.append(pr)
    zq = jnp.zeros_like(rows[0])
    qs = jnp.concatenate([jnp.concatenate([rows[0], zq], axis=-1),
                          jnp.concatenate([zq, rows[1]], axis=-1)], axis=1)
    tab_b = jnp.broadcast_to(rel_bias[:, :, None] * LOG2E, (N_BUCKETS, A_HEADS, LANES))
    page_t = lambda c: c[0].transpose(0, 2, 3, 1).reshape(n_pool, D_KV, PAGE)
    a_out_s = _sample_attention(
        page_table, tab_b, qs, scores, lane_b(thr), lane_b(cut), s_new,
        sp["kf"].reshape(db, 1, D_KV), sp["vf"].reshape(db, 1, D_KV),
        page_t(cache_k), page_t(cache_v), pps)
    unpair = lambda a: a.transpose(1, 0, 2).reshape(db, D_H)
    h_out_s, s_sample = _hgrn_sample(unpair(sp["hq"]), unpair(sp["hg"]), unpair(sp["hk"]),
                                     unpair(sp["hv"]),
                                     state_hgrn[0].transpose(1, 2, 3, 0))
    s_sample = s_sample.transpose(3, 0, 1, 2)
    y_sample = _merge(x_sample.reshape(db, d), a_out_s.reshape(db, D_ATTN), sp["sga"],
                      h_out_s.reshape(db, D_H), sp["sgh"], hng, bd, w_out_pack, db)

    kv5 = lambda a, n: a.reshape(1, n, -1, A_KV_HEADS, HEAD_DIM)
    return (y_prompt.reshape(b, s, d), y_sample.reshape(db, 1, d),
            kv5(pp["kf"], b), kv5(pp["vf"], b),
            pp["ikw"][:, :IDX_DIM].reshape(1, b, s, IDX_DIM), s_prompt[None],
            kv5(sp["kf"], db), kv5(sp["vf"], db),
            ik_s.reshape(1, db, 1, IDX_DIM), s_sample[None])
```
